```python
import jax
import jax.numpy as jnp
from jax import lax
import numpy as np

D_MODEL = 1024
BATCH = 8
SEQ = 2048
DEPTH = 4
DEC_BATCH = 32
DEC_SEQ = 8
PAST_LEN = 8192
PAGE_SIZE = 128

N_HEADS = 16
HEAD_DIM = D_MODEL // N_HEADS
ROPE_THETA = 10000.0
MOBA_BLOCK = 256
MOBA_TOPK = 3
QUERY_CHUNK = 32
D_CONV_A = D_MODEL // 2
D_CONV_B = D_MODEL // 2
CONV_A_WIDTH = 31
CONV_B_WIDTH = 3
N_GROUPS = 4
EXPERTS_PER_GROUP = 4
N_EXPERTS = N_GROUPS * EXPERTS_PER_GROUP
EXPERT_TOP_K = 2
D_EXPERT = D_MODEL // 4
N_AB_LAYERS = (DEPTH + 1) // 2
N_ATTN_LAYERS = DEPTH // 2
NORM_EPS = 1e-6

kernel_name = 'hybrid_conformer_shortconv_moba_hmoe_step'


def rms_norm(x, g):
    xf = x.astype(jnp.float32)
    y = xf * lax.rsqrt(jnp.mean(xf * xf, axis=-1, keepdims=True) + NORM_EPS) * g.astype(jnp.float32)
    return y.astype(x.dtype)


def layer_norm(x, g, b):
    xf = x.astype(jnp.float32)
    mu = jnp.mean(xf, axis=-1, keepdims=True)
    xc = xf - mu
    y = xc * lax.rsqrt(jnp.mean(xc * xc, axis=-1, keepdims=True) + NORM_EPS) * g.astype(jnp.float32) + b.astype(jnp.float32)
    return y.astype(x.dtype)


def rope(x, pos):
    half = HEAD_DIM // 2
    inv = ROPE_THETA ** (-(jnp.arange(half, dtype=jnp.float32) * 2.0 / HEAD_DIM))
    ang = pos.astype(jnp.float32)[:, None] * inv[None, :]
    cos = jnp.cos(ang)[None, :, None, :]
    sin = jnp.sin(ang)[None, :, None, :]
    xf = x.astype(jnp.float32)
    x1, x2 = xf[..., :half], xf[..., half:]
    return jnp.concatenate([x1 * cos - x2 * sin, x2 * cos + x1 * sin], axis=-1).astype(x.dtype)


def causal_dwconv(x, prev, w):
    width = w.shape[0]
    xp = jnp.concatenate([prev.astype(x.dtype), x], axis=1)
    y = lax.conv_general_dilated(xp, w.astype(x.dtype)[:, None, :], (1,), 'VALID',
                                 dimension_numbers=('NWC', 'WIO', 'NWC'),
                                 feature_group_count=x.shape[-1])
    return y, xp[:, xp.shape[1] - (width - 1):]


def conv_mixers(x, prev_a, prev_b, w_in, ca_w, ca_b, ln_g, ln_b, cb_w, w_out):
    u = jnp.einsum('btd,de->bte', x, w_in)
    a_val, a_gate, b_gate, c_gate, h_b = jnp.split(
        u, [D_CONV_A, 2 * D_CONV_A, 2 * D_CONV_A + D_CONV_B, 2 * D_CONV_A + 2 * D_CONV_B], axis=-1)
    a = a_val * jax.nn.sigmoid(a_gate)
    a, new_a = causal_dwconv(a, prev_a, ca_w)
    a = jax.nn.silu(layer_norm(a + ca_b.astype(a.dtype), ln_g, ln_b))
    b, new_b = causal_dwconv(c_gate * h_b, prev_b, cb_w)
    b = b_gate * b
    y = jnp.einsum('bte,ed->btd', jnp.concatenate([a, b], axis=-1), w_out)
    return y, new_a, new_b


def qkv_proj(x, pos, w_qkv, q_g, k_g):
    bsz, t = x.shape[:2]
    qkv = jnp.einsum('btd,de->bte', x, w_qkv).reshape(bsz, t, 3, N_HEADS, HEAD_DIM)
    q = rope(rms_norm(qkv[:, :, 0], q_g), pos)
    k = rope(rms_norm(qkv[:, :, 1], k_g), pos)
    return q, k, qkv[:, :, 2]


def moba_attend(q, k_own, v_own, own_mask, k_sel, v_sel, sel_valid):
    scale = HEAD_DIM ** -0.5
    s_own = jnp.einsum('bhqd,bhkd->bhqk', q, k_own, preferred_element_type=jnp.float32) * scale
    s_own = jnp.where(own_mask, s_own, -jnp.inf)
    if k_sel is None:
        p = jax.nn.softmax(s_own, axis=-1).astype(v_own.dtype)
        out = jnp.einsum('bhqk,bhkd->bhqd', p, v_own, preferred_element_type=jnp.float32)
        return out.astype(q.dtype)
    b, h, nq, n_sel, blk = k_sel.shape[:5]
    s_sel = jnp.einsum('bhqd,bhqjld->bhqjl', q, k_sel, preferred_element_type=jnp.float32) * scale
    s_sel = jnp.where(sel_valid[..., None], s_sel, -jnp.inf)
    n_own = s_own.shape[-1]
    p = jax.nn.softmax(jnp.concatenate([s_own, s_sel.reshape(b, h, nq, n_sel * blk)], axis=-1), axis=-1)
    p = p.astype(v_own.dtype)
    out = (jnp.einsum('bhqk,bhkd->bhqd', p[..., :n_own], v_own, preferred_element_type=jnp.float32)
           + jnp.einsum('bhqjl,bhqjld->bhqd', p[..., n_own:].reshape(b, h, nq, n_sel, blk), v_sel,
                        preferred_element_type=jnp.float32))
    return out.astype(q.dtype)


def moba_prompt(q, k, v):
    bsz, s = q.shape[:2]
    n_blk = -(-s // MOBA_BLOCK)
    pad = n_blk * MOBA_BLOCK - s

    def to_blocks(t):
        t = jnp.pad(t, ((0, 0), (0, pad), (0, 0), (0, 0)))
        return t.reshape(bsz, n_blk, MOBA_BLOCK, N_HEADS, HEAD_DIM).transpose(0, 3, 1, 2, 4)

    kb, vb = to_blocks(k), to_blocks(v)
    k_mean = jnp.mean(kb, axis=3, dtype=jnp.float32)
    n_sel = min(MOBA_TOPK, n_blk - 1)
    n_chunks = s // QUERY_CHUNK
    q_chunks = q.reshape(bsz, n_chunks, QUERY_CHUNK, N_HEADS, HEAD_DIM).transpose(1, 0, 3, 2, 4)
    b_ix = jnp.arange(bsz)[:, None, None, None]
    h_ix = jnp.arange(N_HEADS)[None, :, None, None]

    def one_chunk(args):
        qc, c = args
        q_pos = c * QUERY_CHUNK + jnp.arange(QUERY_CHUNK)
        bq = (c * QUERY_CHUNK) // MOBA_BLOCK
        k_own = lax.dynamic_index_in_dim(kb, bq, axis=2, keepdims=False)
        v_own = lax.dynamic_index_in_dim(vb, bq, axis=2, keepdims=False)
        own_mask = (bq * MOBA_BLOCK + jnp.arange(MOBA_BLOCK))[None, :] <= q_pos[:, None]
        if n_sel == 0:
            return moba_attend(qc, k_own, v_own, own_mask, None, None, None)
        gate = jnp.einsum('bhqd,bhnd->bhqn', qc.astype(jnp.float32), k_mean)
        gate = jnp.where(jnp.arange(n_blk) < bq, gate, -jnp.inf)
        _, idx = lax.top_k(gate, n_sel)
        k_sel = kb[b_ix, h_ix, idx]
        v_sel = vb[b_ix, h_ix, idx]
        return moba_attend(qc, k_own, v_own, own_mask, k_sel, v_sel, idx < bq)

    out = lax.map(one_chunk, (q_chunks, jnp.arange(n_chunks, dtype=jnp.int32)))
    return out.transpose(1, 0, 3, 2, 4).reshape(bsz, s, N_HEADS * HEAD_DIM)


def moba_sample(q, k_new, v_new, k_pool, v_pool, page_table):
    db, t = q.shape[:2]
    past = page_table.shape[1] * PAGE_SIZE
    ppb = MOBA_BLOCK // PAGE_SIZE
    n_full = past // MOBA_BLOCK
    n_part = (past % MOBA_BLOCK) // PAGE_SIZE
    r = n_part * PAGE_SIZE
    part_pages = page_table[:, n_full * ppb:n_full * ppb + n_part]
    k_part = k_pool[part_pages].reshape(db, r, N_HEADS, HEAD_DIM)
    v_part = v_pool[part_pages].reshape(db, r, N_HEADS, HEAD_DIM)
    k_own = jnp.concatenate([k_part, k_new.astype(k_part.dtype)], axis=1).transpose(0, 2, 1, 3)
    v_own = jnp.concatenate([v_part, v_new.astype(v_part.dtype)], axis=1).transpose(0, 2, 1, 3)
    key_ix = jnp.arange(r + t)[None, :]
    own_mask = (key_ix < r) | (key_ix - r <= jnp.arange(t)[:, None])
    qh = q.transpose(0, 2, 1, 3)
    n_sel = min(MOBA_TOPK, n_full)
    if n_sel == 0:
        out = moba_attend(qh, k_own, v_own, own_mask, None, None, None)
    else:
        full_pages = page_table[:, :n_full * ppb]
        k_mean = jnp.mean(k_pool[full_pages].reshape(db, n_full, MOBA_BLOCK, N_HEADS, HEAD_DIM),
                          axis=2, dtype=jnp.float32).transpose(0, 2, 1, 3)
        gate = jnp.einsum('bhqd,bhnd->bhqn', qh.astype(jnp.float32), k_mean)
        _, idx = lax.top_k(gate, n_sel)
        b_ix = jnp.arange(db)[:, None, None, None, None]
        h_ix = jnp.arange(N_HEADS)[None, :, None, None, None]
        phys = page_table[b_ix, idx[..., None] * ppb + jnp.arange(ppb)]
        k_sel = k_pool[phys, :, h_ix, :].reshape(db, N_HEADS, t, n_sel, MOBA_BLOCK, HEAD_DIM)
        v_sel = v_pool[phys, :, h_ix, :].reshape(db, N_HEADS, t, n_sel, MOBA_BLOCK, HEAD_DIM)
        valid = jnp.ones(idx.shape, dtype=bool)
        out = moba_attend(qh, k_own, v_own, own_mask, k_sel, v_sel, valid)
    return out.transpose(0, 2, 1, 3).reshape(db, t, N_HEADS * HEAD_DIM)


def hier_moe(x, wrg, brg, wre, bre, wg, wu, wd):
    bsz, t, d = x.shape
    xf = x.reshape(bsz * t, d)
    n = xf.shape[0]
    lg = jnp.dot(xf, wrg, preferred_element_type=jnp.float32) + brg.astype(jnp.float32)
    pg = jax.nn.softmax(lg, axis=-1)
    grp = jnp.argmax(pg, axis=-1)
    p_grp = jnp.take_along_axis(pg, grp[:, None], axis=1)
    le = (jnp.dot(xf, wre, preferred_element_type=jnp.float32) + bre.astype(jnp.float32))
    le = le.reshape(n, N_GROUPS, EXPERTS_PER_GROUP)
    le_g = jnp.take_along_axis(le, grp[:, None, None], axis=1)[:, 0]
    top_p, top_i = lax.top_k(jax.nn.softmax(le_g, axis=-1), EXPERT_TOP_K)
    weights = p_grp * top_p / jnp.sum(top_p, axis=-1, keepdims=True)
    eid = grp[:, None] * EXPERTS_PER_GROUP + top_i
    combine = jnp.sum(jax.nn.one_hot(eid, N_EXPERTS, dtype=jnp.float32) * weights[..., None], axis=1)
    h = jax.nn.silu(jnp.einsum('nd,edf->nef', xf, wg)) * jnp.einsum('nd,edf->nef', xf, wu)
    y = jnp.einsum('nef,efd->nd', h * combine[:, :, None].astype(h.dtype), wd)
    return y.reshape(bsz, t, d)


def setup_inputs(seed: int = 0) -> dict:
    key = jax.random.key(seed)
    ks = jax.random.split(key, 32)
    f32 = jnp.float32
    n_pages = PAST_LEN // PAGE_SIZE
    n_used = DEC_BATCH * n_pages
    n_pool = n_used + (n_used + 3) // 4
    d_in_ab = 2 * D_CONV_A + 3 * D_CONV_B

    def nrm(k, shape, scale):
        return jax.random.normal(k, shape, f32) * scale

    page_table = jax.random.permutation(ks[4], n_pool)[:n_used].reshape(DEC_BATCH, n_pages).astype(jnp.int32)
    return {
        'x_prompt': nrm(ks[0], (BATCH, SEQ, D_MODEL), 1.0),
        'x_sample': nrm(ks[1], (DEC_BATCH, DEC_SEQ, D_MODEL), 1.0),
        'cache_k': nrm(ks[2], (N_ATTN_LAYERS, n_pool, PAGE_SIZE, N_HEADS, HEAD_DIM), 1.0),
        'cache_v': nrm(ks[3], (N_ATTN_LAYERS, n_pool, PAGE_SIZE, N_HEADS, HEAD_DIM), 1.0),
        'page_table': page_table,
        'state_conv_a': nrm(ks[5], (N_AB_LAYERS, DEC_BATCH, CONV_A_WIDTH - 1, D_CONV_A), 0.5),
        'state_conv_b': nrm(ks[6], (N_AB_LAYERS, DEC_BATCH, CONV_B_WIDTH - 1, D_CONV_B), 1.0),
        'norm_mix_g': 1.0 + nrm(ks[7], (DEPTH, D_MODEL), 0.02),
        'norm_ffn_g': 1.0 + nrm(ks[8], (DEPTH, D_MODEL), 0.02),
        'w_in_ab': nrm(ks[9], (N_AB_LAYERS, D_MODEL, d_in_ab), D_MODEL ** -0.5),
        'conv_a_w': nrm(ks[10], (N_AB_LAYERS, CONV_A_WIDTH, D_CONV_A), CONV_A_WIDTH ** -0.5),
        'conv_a_b': nrm(ks[11], (N_AB_LAYERS, D_CONV_A), 0.02),
        'ln_a_g': 1.0 + nrm(ks[12], (N_AB_LAYERS, D_CONV_A), 0.02),
        'ln_a_b': nrm(ks[13], (N_AB_LAYERS, D_CONV_A), 0.02),
        'conv_b_w': nrm(ks[14], (N_AB_LAYERS, CONV_B_WIDTH, D_CONV_B), CONV_B_WIDTH ** -0.5),
        'w_out_ab': nrm(ks[15], (N_AB_LAYERS, D_CONV_A + D_CONV_B, D_MODEL), (D_CONV_A + D_CONV_B) ** -0.5),
        'w_qkv': nrm(ks[16], (N_ATTN_LAYERS, D_MODEL, 3 * N_HEADS * HEAD_DIM), D_MODEL ** -0.5),
        'q_norm_g': 1.0 + nrm(ks[17], (N_ATTN_LAYERS, HEAD_DIM), 0.02),
        'k_norm_g': 1.0 + nrm(ks[18], (N_ATTN_LAYERS, HEAD_DIM), 0.02),
        'w_o': nrm(ks[19], (N_ATTN_LAYERS, N_HEADS * HEAD_DIM, D_MODEL), (N_HEADS * HEAD_DIM) ** -0.5),
        'router_g_w': nrm(ks[20], (DEPTH, D_MODEL, N_GROUPS), D_MODEL ** -0.5),
        'router_g_b': nrm(ks[21], (DEPTH, N_GROUPS), 0.01),
        'router_e_w': nrm(ks[22], (DEPTH, D_MODEL, N_EXPERTS), D_MODEL ** -0.5),
        'router_e_b': nrm(ks[23], (DEPTH, N_EXPERTS), 0.01),
        'w_gate': nrm(ks[24], (DEPTH, N_EXPERTS, D_MODEL, D_EXPERT), D_MODEL ** -0.5),
        'w_up': nrm(ks[25], (DEPTH, N_EXPERTS, D_MODEL, D_EXPERT), D_MODEL ** -0.5),
        'w_down': nrm(ks[26], (DEPTH, N_EXPERTS, D_EXPERT, D_MODEL), D_EXPERT ** -0.5),
    }


def reference(x_prompt, x_sample, cache_k, cache_v, page_table, state_conv_a, state_conv_b,
              norm_mix_g, norm_ffn_g, w_in_ab, conv_a_w, conv_a_b, ln_a_g, ln_a_b, conv_b_w, w_out_ab,
              w_qkv, q_norm_g, k_norm_g, w_o, router_g_w, router_g_b, router_e_w, router_e_b,
              w_gate, w_up, w_down):
    s = x_prompt.shape[1]
    t = x_sample.shape[1]
    past = page_table.shape[1] * PAGE_SIZE
    pos_p = jnp.arange(s, dtype=jnp.int32)
    pos_s = past + jnp.arange(t, dtype=jnp.int32)
    hp, hs = x_prompt, x_sample
    kp_l, vp_l, ks_l, vs_l = [], [], [], []
    cap_l, cbp_l, cas_l, cbs_l = [], [], [], []
    for layer in range(DEPTH):
        j = layer // 2
        if layer % 2 == 0:
            ab = (w_in_ab[j], conv_a_w[j], conv_a_b[j], ln_a_g[j], ln_a_b[j], conv_b_w[j], w_out_ab[j])
            zero_a = jnp.zeros((hp.shape[0], CONV_A_WIDTH - 1, D_CONV_A), hp.dtype)
            zero_b = jnp.zeros((hp.shape[0], CONV_B_WIDTH - 1, D_CONV_B), hp.dtype)
            yp, a_p, b_p = conv_mixers(rms_norm(hp, norm_mix_g[layer]), zero_a, zero_b, *ab)
            ys, a_s, b_s = conv_mixers(rms_norm(hs, norm_mix_g[layer]), state_conv_a[j], state_conv_b[j], *ab)
            cap_l.append(a_p)
            cbp_l.append(b_p)
            cas_l.append(a_s)
            cbs_l.append(b_s)
        else:
            q, k, v = qkv_proj(rms_norm(hp, norm_mix_g[layer]), pos_p, w_qkv[j], q_norm_g[j], k_norm_g[j])
            yp = jnp.einsum('bte,ed->btd', moba_prompt(q, k, v), w_o[j])
            kp_l.append(k.reshape(-1, PAGE_SIZE, N_HEADS, HEAD_DIM))
            vp_l.append(v.reshape(-1, PAGE_SIZE, N_HEADS, HEAD_DIM))
            q, k, v = qkv_proj(rms_norm(hs, norm_mix_g[layer]), pos_s, w_qkv[j], q_norm_g[j], k_norm_g[j])
            ys = jnp.einsum('bte,ed->btd', moba_sample(q, k, v, cache_k[j], cache_v[j], page_table), w_o[j])
            ks_l.append(k)
            vs_l.append(v)
        hp = hp + yp
        hs = hs + ys
        moe = (router_g_w[layer], router_g_b[layer], router_e_w[layer], router_e_b[layer],
               w_gate[layer], w_up[layer], w_down[layer])
        hp = hp + hier_moe(rms_norm(hp, norm_ffn_g[layer]), *moe)
        hs = hs + hier_moe(rms_norm(hs, norm_ffn_g[layer]), *moe)
    return (hp, hs, jnp.stack(kp_l), jnp.stack(vp_l), jnp.stack(ks_l), jnp.stack(vs_l),
            jnp.stack(cap_l), jnp.stack(cbp_l), jnp.stack(cas_l), jnp.stack(cbs_l))
```

```python
import functools

import jax
import jax.numpy as jnp
from jax import lax
from jax.experimental import pallas as pl
from jax.experimental.pallas import tpu as pltpu

F32 = jnp.float32
BF16 = jnp.bfloat16

D_MODEL = 1024
N_HEADS = 16
HEAD_DIM = 64
N_PAIRS = N_HEADS // 2
ROPE_THETA = 10000.0
MOBA_BLOCK = 256
MOBA_TOPK = 3
PAGE_SIZE = 128
D_CONV = 512
CONV_A_WIDTH = 31
CONV_B_WIDTH = 3
HALO_A = 32
HALO_B = 8
N_GROUPS = 4
EXPERTS_PER_GROUP = 4
N_EXPERTS = 16
D_EXPERT = 256
NORM_EPS = 1e-6
NEG_INF = float("-inf")
LANES = 128
VMEM_LIMIT = 56 * 1024 * 1024
PAGE_SLOTS = 4
HI_PRECISION_LAYERS = 2


def _dot(a, b):
    return jnp.dot(a, b, preferred_element_type=F32)


def _dot_nt(a, b):
    return lax.dot_general(a, b, (((1,), (1,)), ((), ())), preferred_element_type=F32)


def _mm(x, w_ref, cols=slice(None)):
    x_hi = x.astype(BF16)
    y = _dot(x_hi, w_ref[0, :, cols])
    if w_ref.shape[0] == 2:
        x_lo = (x - x_hi.astype(F32)).astype(BF16)
        y = y + _dot(x_lo, w_ref[0, :, cols]) + _dot(x_hi, w_ref[1, :, cols])
    return y


def _planes(w, hi):
    w_hi = w.astype(BF16)
    if not hi:
        return w_hi[..., None, :, :]
    w_lo = (w - w_hi.astype(F32)).astype(BF16)
    return jnp.stack([w_hi, w_lo], axis=-3)


def _rms(x, g):
    return x * lax.rsqrt(jnp.mean(x * x, axis=-1, keepdims=True) + NORM_EPS) * g


def _sigmoid(x):
    return 1.0 / (1.0 + jnp.exp(-x))


def _params(*sem):
    return pltpu.CompilerParams(dimension_semantics=sem, vmem_limit_bytes=VMEM_LIMIT)


def _ab_kernel(h_ref, pa_ref, pb_ref, g_ref, win_ref, caw_ref, cab_ref, lng_ref, lnb_ref, cbw_ref, wout_ref,
               o_ref, na_ref, nb_ref, sa_ref, sb_ref, ab_ref, *, bb, t, r, n_tiles):
    i = pl.program_id(1)
    n = bb * t

    @pl.when(i == 0)
    def _():
        sa_ref[:, 0:HALO_A, :] = pa_ref[...]
        sb_ref[:, 0:HALO_B, :] = pb_ref[...]

    x = h_ref[...].reshape(n, D_MODEL)
    xn = _rms(x, g_ref[...])
    a = _mm(xn, win_ref, slice(0, 512)) * _sigmoid(_mm(xn, win_ref, slice(512, 1024)))
    sa_ref[:, HALO_A:HALO_A + t, :] = a.reshape(bb, t, D_CONV)
    bg = _mm(xn, win_ref, slice(1024, 1536))
    ch = _mm(xn, win_ref, slice(1536, 2048)) * _mm(xn, win_ref, slice(2048, 2560))
    sb_ref[:, HALO_B:HALO_B + t, :] = ch.reshape(bb, t, D_CONV)

    off_b = HALO_B - (CONV_B_WIDTH - 1)
    cb = cbw_ref[0:1, :] * sb_ref[:, off_b:off_b + t, :]
    for k in range(1, CONV_B_WIDTH):
        cb = cb + cbw_ref[k:k + 1, :] * sb_ref[:, off_b + k:off_b + k + t, :]
    ab_ref[:, D_CONV:2 * D_CONV] = bg * cb.reshape(n, D_CONV)

    off_a = HALO_A - (CONV_A_WIDTH - 1)
    for b in range(bb):
        for c in range(t // r):
            r0 = c * r + off_a
            acc = caw_ref[0:1, :] * sa_ref[b, r0:r0 + r, :]
            for k in range(1, CONV_A_WIDTH):
                acc = acc + caw_ref[k:k + 1, :] * sa_ref[b, r0 + k:r0 + k + r, :]
            acc = acc + cab_ref[...]
            mu = jnp.mean(acc, axis=-1, keepdims=True)
            xc = acc - mu
            y = xc * lax.rsqrt(jnp.mean(xc * xc, axis=-1, keepdims=True) + NORM_EPS) * lng_ref[...] + lnb_ref[...]
            ab_ref[b * t + c * r:b * t + c * r + r, 0:D_CONV] = y * _sigmoid(y)

    y = _mm(ab_ref[...], wout_ref)
    o_ref[...] = (x + y).reshape(bb, t, D_MODEL)

    na_ref[...] = sa_ref[:, t:t + HALO_A, :]
    nb_ref[...] = sb_ref[:, t:t + HALO_B, :]
    if n_tiles > 1:
        sa_ref[:, 0:HALO_A, :] = sa_ref[:, t:t + HALO_A, :]
        sb_ref[:, 0:HALO_B, :] = sb_ref[:, t:t + HALO_B, :]


def _ab_call(h, prev_a, prev_b, layer, j, norm_g, w_in, ca_w, ca_b, ln_g, ln_b, cb_w, w_out, *, bb, t, planes):
    bsz, s, _ = h.shape
    n_tiles = s // t
    r = min(32, t)
    d_in = w_in.shape[-1]
    kern = functools.partial(_ab_kernel, bb=bb, t=t, r=r, n_tiles=n_tiles)
    return pl.pallas_call(
        kern,
        grid=(bsz // bb, n_tiles),
        in_specs=[
            pl.BlockSpec((bb, t, D_MODEL), lambda b, i: (b, i, 0)),
            pl.BlockSpec((bb, HALO_A, D_CONV), lambda b, i: (b, 0, 0)),
            pl.BlockSpec((bb, HALO_B, D_CONV), lambda b, i: (b, 0, 0)),
            pl.BlockSpec((None, 1, D_MODEL), lambda b, i: (layer, 0, 0)),
            pl.BlockSpec((None, planes, D_MODEL, d_in), lambda b, i: (j, 0, 0, 0)),
            pl.BlockSpec((None, CONV_A_WIDTH, D_CONV), lambda b, i: (j, 0, 0)),
            pl.BlockSpec((None, 1, D_CONV), lambda b, i: (j, 0, 0)),
            pl.BlockSpec((None, 1, D_CONV), lambda b, i: (j, 0, 0)),
            pl.BlockSpec((None, 1, D_CONV), lambda b, i: (j, 0, 0)),
            pl.BlockSpec((None, CONV_B_WIDTH, D_CONV), lambda b, i: (j, 0, 0)),
            pl.BlockSpec((None, planes, 2 * D_CONV, D_MODEL), lambda b, i: (j, 0, 0, 0)),
        ],
        out_specs=[
            pl.BlockSpec((bb, t, D_MODEL), lambda b, i: (b, i, 0)),
            pl.BlockSpec((bb, HALO_A, D_CONV), lambda b, i: (b, 0, 0)),
            pl.BlockSpec((bb, HALO_B, D_CONV), lambda b, i: (b, 0, 0)),
        ],
        out_shape=[
            jax.ShapeDtypeStruct(h.shape, F32),
            jax.ShapeDtypeStruct((bsz, HALO_A, D_CONV), F32),
            jax.ShapeDtypeStruct((bsz, HALO_B, D_CONV), F32),
        ],
        scratch_shapes=[
            pltpu.VMEM((bb, t + HALO_A, D_CONV), F32),
            pltpu.VMEM((bb, t + HALO_B, D_CONV), F32),
            pltpu.VMEM((bb * t, 2 * D_CONV), F32),
        ],
        compiler_params=_params("arbitrary", "arbitrary"),
        name="ab_mixers",
    )(h, prev_a, prev_b, norm_g, w_in, ca_w, ca_b, ln_g, ln_b, cb_w, w_out)


def _qkv_kernel(h_ref, g_ref, w_ref, qg_ref, kg_ref, cos_ref, sin_ref, q_ref, k_ref, v_ref, *, t):
    xn = _rms(h_ref[...], g_ref[...])
    lane = lax.broadcasted_iota(jnp.int32, (t, LANES), 1)
    lo = lane < HEAD_DIM
    first = (lane & (HEAD_DIM - 1)) < HEAD_DIM // 2
    cos = cos_ref[...]
    sin = sin_ref[...]

    def norm_rope(col, gg):
        sq = col * col
        s0 = jnp.sum(jnp.where(lo, sq, 0.0), axis=-1, keepdims=True)
        s1 = jnp.sum(jnp.where(lo, 0.0, sq), axis=-1, keepdims=True)
        ms = jnp.where(lo, s0, s1) * (1.0 / HEAD_DIM)
        y = col * lax.rsqrt(ms + NORM_EPS) * gg
        rot = jnp.where(first, pltpu.roll(y, LANES - HEAD_DIM // 2, 1), pltpu.roll(y, HEAD_DIM // 2, 1))
        return y * cos + rot * sin

    qa = _mm(xn, w_ref, slice(0, D_MODEL))
    for c in range(N_PAIRS):
        q_ref[:, c * LANES:(c + 1) * LANES] = norm_rope(qa[:, c * LANES:(c + 1) * LANES], qg_ref[...])
    ka = _mm(xn, w_ref, slice(D_MODEL, 2 * D_MODEL))
    for c in range(N_PAIRS):
        k_ref[:, c * LANES:(c + 1) * LANES] = norm_rope(ka[:, c * LANES:(c + 1) * LANES], kg_ref[...])
    v_ref[...] = _mm(xn, w_ref, slice(2 * D_MODEL, 3 * D_MODEL))


def _qkv_call(h2, layer, j, norm_g, w_qkv, qg, kg, cos_t, sin_t, *, t, planes):
    n = h2.shape[0]
    n_pos_tiles = cos_t.shape[0] // t
    kern = functools.partial(_qkv_kernel, t=t)
    tok = pl.BlockSpec((t, D_MODEL), lambda i: (i, 0))
    tab = pl.BlockSpec((t, LANES), lambda i: (i % n_pos_tiles, 0))
    return pl.pallas_call(
        kern,
        grid=(n // t,),
        in_specs=[
            tok,
            pl.BlockSpec((None, 1, D_MODEL), lambda i: (layer, 0, 0)),
            pl.BlockSpec((None, planes, D_MODEL, 3 * D_MODEL), lambda i: (j, 0, 0, 0)),
            pl.BlockSpec((None, 1, LANES), lambda i: (j, 0, 0)),
            pl.BlockSpec((None, 1, LANES), lambda i: (j, 0, 0)),
            tab, tab,
        ],
        out_specs=[tok, tok, tok],
        out_shape=[jax.ShapeDtypeStruct((n, D_MODEL), F32)] * 3,
        compiler_params=_params("arbitrary"),
        name="qkv_rope",
    )(h2, norm_g, w_qkv, qg, kg, cos_t, sin_t)


def _top_blocks(gate, lanef):
    sel = jnp.zeros(gate.shape, F32)
    for _ in range(MOBA_TOPK):
        mx = jnp.max(gate, axis=-1, keepdims=True)
        idx = jnp.min(jnp.where(gate == mx, lanef, 1e9), axis=-1, keepdims=True)
        pick = (lanef == idx) & (mx > NEG_INF)
        sel = jnp.where(pick, 1.0, sel)
        gate = jnp.where(pick, NEG_INF, gate)
    return sel


def _attn_p_kernel(q_ref, k_ref, v_ref, o_ref, kb_s, vb_s, km_s, m_s, l_s, acc_s, *, n_blk):
    i = pl.program_id(2)
    blk = MOBA_BLOCK

    @pl.when(i == 0)
    def _():
        kb_s[...] = k_ref[0].astype(BF16)
        vb_s[...] = v_ref[0].astype(BF16)
        km_s[...] = jnp.zeros(km_s.shape, F32)
        for n in range(n_blk):
            km_s[n:n + 1, :] = jnp.mean(k_ref[0, n * blk:(n + 1) * blk, :], axis=0, keepdims=True)

    q = q_ref[0] * (HEAD_DIM ** -0.5)
    lane = lax.broadcasted_iota(jnp.int32, (blk, LANES), 1)
    lanef = lane.astype(F32)
    causal = (lax.broadcasted_iota(jnp.int32, (blk, blk), 1) <= lax.broadcasted_iota(jnp.int32, (blk, blk), 0))
    start = pl.multiple_of(i * blk, blk)
    outs = []
    for j in range(2):
        qjf = jnp.where((lane < HEAD_DIM) if j == 0 else (lane >= HEAD_DIM), q, 0.0)
        qj = qjf.astype(BF16)
        s = jnp.where(causal, _dot_nt(qj, kb_s[pl.ds(start, blk), :]), NEG_INF)
        m = jnp.max(s, axis=-1, keepdims=True)
        p = jnp.exp(s - m)
        m_s[...] = m
        l_s[...] = jnp.sum(p, axis=-1, keepdims=True)
        acc_s[...] = _dot(p.astype(BF16), vb_s[pl.ds(start, blk), :])
        gate = lax.dot_general(qjf, km_s[...], (((1,), (1,)), ((), ())), precision=lax.Precision.HIGHEST,
                               preferred_element_type=F32)
        sel = _top_blocks(jnp.where(lane < i, gate, NEG_INF), lanef)
        for n in range(n_blk - 1):
            @pl.when(n < i)
            def _(n=n):
                col = jnp.max(jnp.where(lane == n, sel, 0.0), axis=-1, keepdims=True)
                sn = jnp.where(col > 0.0, _dot_nt(qj, kb_s[n * blk:(n + 1) * blk, :]), NEG_INF)
                m_old = m_s[...]
                m_new = jnp.maximum(m_old, jnp.max(sn, axis=-1, keepdims=True))
                alpha = jnp.exp(m_old - m_new)
                pn = jnp.exp(sn - m_new)
                l_s[...] = alpha * l_s[...] + jnp.sum(pn, axis=-1, keepdims=True)
                acc_s[...] = alpha * acc_s[...] + _dot(pn.astype(BF16), vb_s[n * blk:(n + 1) * blk, :])
                m_s[...] = m_new
        outs.append(acc_s[...] / l_s[...])
    o_ref[0] = jnp.where(lane < HEAD_DIM, outs[0], outs[1]).astype(o_ref.dtype)


def _attn_p_call(q, k, v, out_dtype):
    bsz, s, _ = q.shape
    n_blk = s // MOBA_BLOCK
    assert n_blk <= LANES
    kern = functools.partial(_attn_p_kernel, n_blk=n_blk)
    qs = pl.BlockSpec((1, MOBA_BLOCK, LANES), lambda b, p, i: (b, i, p))
    kvs = pl.BlockSpec((1, s, LANES), lambda b, p, i: (b, 0, p))
    return pl.pallas_call(
        kern,
        grid=(bsz, N_PAIRS, n_blk),
        in_specs=[qs, kvs, kvs],
        out_specs=qs,
        out_shape=jax.ShapeDtypeStruct((bsz, s, D_MODEL), out_dtype),
        scratch_shapes=[
            pltpu.VMEM((s, LANES), BF16),
            pltpu.VMEM((s, LANES), BF16),
            pltpu.VMEM((LANES, LANES), F32),
            pltpu.VMEM((MOBA_BLOCK, 1), F32),
            pltpu.VMEM((MOBA_BLOCK, 1), F32),
            pltpu.VMEM((MOBA_BLOCK, LANES), F32),
        ],
        compiler_params=_params("arbitrary", "arbitrary", "arbitrary"),
        name="moba_prompt",
    )(q, k, v)


def _attn_s_kernel(pt_ref, q_ref, kn_ref, vn_ref, ck_hbm, cv_hbm, o_ref,
                   kbuf, vbuf, ksem, vsem, qbd_s, m_s, l_s, g_s, acc_s, *, layer, n_pages, t):
    b = pl.program_id(0)
    rows = 2 * t

    def k_copy(pg, slot):
        return pltpu.make_async_copy(ck_hbm.at[layer, pt_ref[b, pg]], kbuf.at[slot], ksem.at[slot])

    def v_copy(pg, slot):
        return pltpu.make_async_copy(cv_hbm.at[layer, pt_ref[b, pg]], vbuf.at[slot], vsem.at[slot])

    for pg in range(PAGE_SLOTS):
        k_copy(pg, pg).start()
        v_copy(pg, pg).start()

    lane = lax.broadcasted_iota(jnp.int32, (rows, LANES), 1)
    row = lax.broadcasted_iota(jnp.int32, (rows, LANES), 0)
    lanef = lane.astype(F32)
    head_mask = (lane < HEAD_DIM) == (row < t)
    q = q_ref[0] * (HEAD_DIM ** -0.5)
    qf = []
    for hp in range(N_PAIRS):
        qc = q[:, hp * LANES:(hp + 1) * LANES]
        qf.append(jnp.where(head_mask, jnp.concatenate([qc, qc], axis=0), 0.0))
        qbd_s[hp] = qf[hp].astype(BF16)
    m_s[...] = jnp.full(m_s.shape, NEG_INF, F32)
    l_s[...] = jnp.zeros(l_s.shape, F32)
    g_s[...] = jnp.zeros(g_s.shape, F32)

    def page_body(pg, carry):
        slot = lax.rem(pg, PAGE_SLOTS)
        k_copy(pg, slot).wait()
        v_copy(pg, slot).wait()
        here = lane == pg
        for hp in range(N_PAIRS):
            kp = kbuf[slot, hp * LANES:(hp + 1) * LANES, :].astype(BF16)
            vp = vbuf[slot, hp * LANES:(hp + 1) * LANES, :].astype(BF16)
            s = _dot(qbd_s[hp], kp)
            m = jnp.max(s, axis=-1, keepdims=True)
            p = jnp.exp(s - m)
            m_s[hp] = jnp.where(here, m, m_s[hp])
            l_s[hp] = jnp.where(here, jnp.sum(p, axis=-1, keepdims=True), l_s[hp])
            g_s[hp] = jnp.where(here, jnp.sum(s, axis=-1, keepdims=True), g_s[hp])
            acc_s[pg, hp] = _dot_nt(p.astype(BF16), vp)

        @pl.when(pg + PAGE_SLOTS < n_pages)
        def _():
            k_copy(pg + PAGE_SLOTS, slot).start()
            v_copy(pg + PAGE_SLOTS, slot).start()
        return carry

    lax.fori_loop(0, n_pages, page_body, 0)

    kn = kn_ref[0]
    vn = vn_ref[0]
    t_row = (row[:, 0:1] & (t - 1))
    pages_per_blk = MOBA_BLOCK // PAGE_SIZE
    cols = []
    for hp in range(N_PAIRS):
        kc = kn[:, hp * LANES:(hp + 1) * LANES]
        vc = vn[:, hp * LANES:(hp + 1) * LANES]
        s_own = []
        for t2 in range(t):
            st = jnp.sum(qf[hp] * kc[t2:t2 + 1, :], axis=-1, keepdims=True)
            s_own.append(jnp.where(t_row >= t2, st, NEG_INF))
        m_own = s_own[0]
        for t2 in range(1, t):
            m_own = jnp.maximum(m_own, s_own[t2])
        l_own = jnp.zeros((rows, 1), F32)
        acc_own = jnp.zeros((rows, LANES), F32)
        for t2 in range(t):
            pt2 = jnp.exp(s_own[t2] - m_own)
            l_own = l_own + pt2
            acc_own = acc_own + pt2 * vc[t2:t2 + 1, :]
        gs = g_s[hp]
        gblk = gs + pltpu.roll(gs, LANES - 1, 1)
        cand = ((lane & (pages_per_blk - 1)) == 0) & (lane < n_pages)
        sel = _top_blocks(jnp.where(cand, gblk, NEG_INF), lanef)
        selp = (sel + pltpu.roll(sel, 1, 1)) > 0.0
        mm = m_s[hp]
        big = jnp.maximum(jnp.max(jnp.where(selp, mm, NEG_INF), axis=-1, keepdims=True), m_own)
        w = jnp.where(selp, jnp.exp(mm - big), 0.0)
        w_own = jnp.exp(m_own - big)
        den = jnp.sum(w * l_s[hp], axis=-1, keepdims=True) + w_own * l_own
        num = w_own * acc_own
        for pg in range(n_pages):
            num = num + w[:, pg:pg + 1] * acc_s[pg, hp]
        o = num / den
        cols.append(jnp.where(lane[0:t] < HEAD_DIM, o[0:t], o[t:rows]))
    o_ref[0] = jnp.concatenate(cols, axis=1)


def _attn_s_call(page_table, q, k_new, v_new, ck, cv, layer):
    db, t, _ = q.shape
    n_pages = page_table.shape[1]
    assert (n_pages * PAGE_SIZE) % MOBA_BLOCK == 0 and n_pages <= LANES and PAGE_SLOTS <= n_pages
    assert t & (t - 1) == 0
    kern = functools.partial(_attn_s_kernel, layer=layer, n_pages=n_pages, t=t)
    tok = pl.BlockSpec((1, t, D_MODEL), lambda b, pt: (b, 0, 0))
    rows = 2 * t
    grid_spec = pltpu.PrefetchScalarGridSpec(
        num_scalar_prefetch=1,
        grid=(db,),
        in_specs=[tok, tok, tok, pl.BlockSpec(memory_space=pl.ANY), pl.BlockSpec(memory_space=pl.ANY)],
        out_specs=tok,
        scratch_shapes=[
            pltpu.VMEM((PAGE_SLOTS, D_MODEL, PAGE_SIZE), F32),
            pltpu.VMEM((PAGE_SLOTS, D_MODEL, PAGE_SIZE), F32),
            pltpu.SemaphoreType.DMA((PAGE_SLOTS,)),
            pltpu.SemaphoreType.DMA((PAGE_SLOTS,)),
            pltpu.VMEM((N_PAIRS, rows, LANES), BF16),
            pltpu.VMEM((N_PAIRS, rows, LANES), F32),
            pltpu.VMEM((N_PAIRS, rows, LANES), F32),
            pltpu.VMEM((N_PAIRS, rows, LANES), F32),
            pltpu.VMEM((n_pages, N_PAIRS, rows, LANES), F32),
        ],
    )
    return pl.pallas_call(
        kern,
        grid_spec=grid_spec,
        out_shape=jax.ShapeDtypeStruct((db, t, D_MODEL), F32),
        compiler_params=_params("arbitrary"),
        name="moba_sample",
    )(page_table, q, k_new, v_new, ck, cv)


def _route(logits):
    lane = lax.broadcasted_iota(jnp.int32, logits.shape, 1)
    lanef = lane.astype(F32)
    is_g = (lane >= N_EXPERTS) & (lane < N_EXPERTS + N_GROUPS)
    lg = jnp.where(is_g, logits, NEG_INF)
    mg = jnp.max(lg, axis=-1, keepdims=True)
    grp = jnp.min(jnp.where(lg == mg, lanef, 1e9), axis=-1, keepdims=True) - float(N_EXPERTS)
    p_grp = 1.0 / jnp.sum(jnp.exp(lg - mg), axis=-1, keepdims=True)
    member = (lane < N_EXPERTS) & (lax.shift_right_logical(lane, 2).astype(F32) == grp)
    le = jnp.where(member, logits, NEG_INF)
    m1 = jnp.max(le, axis=-1, keepdims=True)
    i1 = jnp.min(jnp.where(le == m1, lanef, 1e9), axis=-1, keepdims=True)
    le2 = jnp.where(lanef == i1, NEG_INF, le)
    m2 = jnp.max(le2, axis=-1, keepdims=True)
    i2 = jnp.min(jnp.where(le2 == m2, lanef, 1e9), axis=-1, keepdims=True)
    e2 = jnp.exp(m2 - m1)
    den = 1.0 + e2
    return jnp.where(lanef == i1, p_grp / den, 0.0) + jnp.where(lanef == i2, p_grp * e2 / den, 0.0)


def _moe_kernel(*refs, has_proj, tm):
    if has_proj:
        h_ref, att_ref, wo_ref = refs[:3]
        refs = refs[3:]
    else:
        h_ref = refs[0]
        refs = refs[1:]
    g_ref, wr_ref, br_ref, wg_ref, wu_ref, wd_ref, o_ref, xn_s, comb_s, acc_s = refs
    gi = pl.program_id(1)

    @pl.when(gi == 0)
    def _():
        hm = h_ref[...]
        if has_proj:
            hm = hm + _mm(att_ref[...].astype(F32), wo_ref)
        acc_s[...] = hm
        xn = _rms(hm, g_ref[...])
        xn_s[...] = xn.astype(BF16)
        logits = jnp.dot(xn, wr_ref[...], precision=lax.Precision.HIGHEST, preferred_element_type=F32)
        comb_s[...] = _route(logits + br_ref[...])

    x = xn_s[...]
    lane = lax.broadcasted_iota(jnp.int32, (tm, LANES), 1)
    comb = comb_s[...]
    for e in range(EXPERTS_PER_GROUP):
        c = jnp.sum(jnp.where(lane == gi * EXPERTS_PER_GROUP + e, comb, 0.0), axis=-1, keepdims=True)
        hg = _dot(x, wg_ref[e])
        hh = hg * _sigmoid(hg) * _dot(x, wu_ref[e]) * c
        acc_s[...] += _dot(hh.astype(BF16), wd_ref[e])

    @pl.when(gi == N_GROUPS - 1)
    def _():
        o_ref[...] = acc_s[...]


def _moe_call(h2, layer, norm_g, wr, br, wg, wu, wd, att=None, w_o=None, j=0, planes=1, *, tm):
    n = h2.shape[0]
    has_proj = att is not None
    kern = functools.partial(_moe_kernel, has_proj=has_proj, tm=tm)
    tok = pl.BlockSpec((tm, D_MODEL), lambda i, g: (i, 0))
    in_specs = [tok]
    args = [h2]
    if has_proj:
        in_specs += [tok, pl.BlockSpec((None, planes, D_MODEL, D_MODEL), lambda i, g: (j, 0, 0, 0))]
        args += [att, w_o]
    in_specs += [
        pl.BlockSpec((None, 1, D_MODEL), lambda i, g: (layer, 0, 0)),
        pl.BlockSpec((None, D_MODEL, LANES), lambda i, g: (layer, 0, 0)),
        pl.BlockSpec((None, 1, LANES), lambda i, g: (layer, 0, 0)),
        pl.BlockSpec((None, None, EXPERTS_PER_GROUP, D_MODEL, D_EXPERT), lambda i, g: (layer, g, 0, 0, 0)),
        pl.BlockSpec((None, None, EXPERTS_PER_GROUP, D_MODEL, D_EXPERT), lambda i, g: (layer, g, 0, 0, 0)),
        pl.BlockSpec((None, None, EXPERTS_PER_GROUP, D_EXPERT, D_MODEL), lambda i, g: (layer, g, 0, 0, 0)),
    ]
    args += [norm_g, wr, br, wg, wu, wd]
    return pl.pallas_call(
        kern,
        grid=(n // tm, N_GROUPS),
        in_specs=in_specs,
        out_specs=tok,
        out_shape=jax.ShapeDtypeStruct((n, D_MODEL), F32),
        scratch_shapes=[
            pltpu.VMEM((tm, D_MODEL), BF16),
            pltpu.VMEM((tm, LANES), F32),
            pltpu.VMEM((tm, D_MODEL), F32),
        ],
        compiler_params=_params("arbitrary", "arbitrary"),
        name="moe_proj" if has_proj else "moe",
    )(*args)


def _rope_tables(pos):
    half = HEAD_DIM // 2
    inv = ROPE_THETA ** (-(jnp.arange(half, dtype=F32) * 2.0 / HEAD_DIM))
    ang = pos.astype(F32)[:, None] * inv[None, :]
    cos = jnp.cos(ang)
    sin = jnp.sin(ang)
    cos_h = jnp.concatenate([cos, cos], axis=-1)
    sin_h = jnp.concatenate([-sin, sin], axis=-1)
    return jnp.concatenate([cos_h, cos_h], axis=-1), jnp.concatenate([sin_h, sin_h], axis=-1)


def kernel(x_prompt, x_sample, cache_k, cache_v, page_table, state_conv_a, state_conv_b, norm_mix_g, norm_ffn_g,
           w_in_ab, conv_a_w, conv_a_b, ln_a_g, ln_a_b, conv_b_w, w_out_ab, w_qkv, q_norm_g, k_norm_g, w_o,
           router_g_w, router_g_b, router_e_w, router_e_b, w_gate, w_up, w_down):
    bsz, s, _ = x_prompt.shape
    db, t, _ = x_sample.shape
    depth = norm_mix_g.shape[0]
    n_pages = page_table.shape[1]
    past = n_pages * PAGE_SIZE
    n_pool = cache_k.shape[1]

    w_in_b = _planes(w_in_ab, True)
    w_out_b = _planes(w_out_ab, True)
    w_qkv_b = _planes(w_qkv, True)
    w_o_b = _planes(w_o, True)
    wg_b = w_gate.astype(BF16).reshape(depth, N_GROUPS, EXPERTS_PER_GROUP, D_MODEL, D_EXPERT)
    wu_b = w_up.astype(BF16).reshape(depth, N_GROUPS, EXPERTS_PER_GROUP, D_MODEL, D_EXPERT)
    wd_b = w_down.astype(BF16).reshape(depth, N_GROUPS, EXPERTS_PER_GROUP, D_EXPERT, D_MODEL)
    pad = LANES - N_EXPERTS - N_GROUPS
    wr = jnp.concatenate([router_e_w, router_g_w, jnp.zeros((depth, D_MODEL, pad), F32)], axis=-1)
    br = jnp.concatenate([router_e_b, router_g_b, jnp.zeros((depth, pad), F32)], axis=-1)[:, None, :]
    qg = jnp.concatenate([q_norm_g, q_norm_g], axis=-1)[:, None, :]
    kg = jnp.concatenate([k_norm_g, k_norm_g], axis=-1)[:, None, :]
    norm_mix_g = norm_mix_g[:, None, :]
    norm_ffn_g = norm_ffn_g[:, None, :]
    conv_a_b = conv_a_b[:, None, :]
    ln_a_g = ln_a_g[:, None, :]
    ln_a_b = ln_a_b[:, None, :]
    cos_p, sin_p = _rope_tables(jnp.arange(s, dtype=jnp.int32))
    cos_s, sin_s = _rope_tables(past + jnp.arange(t, dtype=jnp.int32))
    cos_s = jnp.tile(cos_s, (db, 1))
    sin_s = jnp.tile(sin_s, (db, 1))
    ck = jnp.transpose(cache_k, (0, 1, 3, 4, 2)).reshape(cache_k.shape[0], n_pool, D_MODEL, PAGE_SIZE)
    cv = jnp.transpose(cache_v, (0, 1, 3, 4, 2)).reshape(cache_v.shape[0], n_pool, D_MODEL, PAGE_SIZE)

    hp, hs = x_prompt, x_sample
    kp_l, vp_l, ks_l, vs_l = [], [], [], []
    cap_l, cbp_l, cas_l, cbs_l = [], [], [], []
    moe_w = (norm_ffn_g, wr, br, wg_b, wu_b, wd_b)
    for layer in range(depth):
        j = layer // 2
        pp = 2 if layer < HI_PRECISION_LAYERS else 1
        if layer % 2 == 0:
            ab_w = (norm_mix_g, w_in_b, conv_a_w, conv_a_b, ln_a_g, ln_a_b, conv_b_w, w_out_b)
            zero_a = jnp.zeros((bsz, HALO_A, D_CONV), F32)
            zero_b = jnp.zeros((bsz, HALO_B, D_CONV), F32)
            hp, a_p, b_p = _ab_call(hp, zero_a, zero_b, layer, j, *ab_w, bb=1, t=256, planes=pp)
            prev_a = jnp.pad(state_conv_a[j], ((0, 0), (HALO_A - (CONV_A_WIDTH - 1), 0), (0, 0)))
            prev_b = jnp.pad(state_conv_b[j], ((0, 0), (HALO_B - (CONV_B_WIDTH - 1), 0), (0, 0)))
            hs, a_s, b_s = _ab_call(hs, prev_a, prev_b, layer, j, *ab_w, bb=db, t=t, planes=1)
            cap_l.append(a_p[:, HALO_A - (CONV_A_WIDTH - 1):])
            cbp_l.append(b_p[:, HALO_B - (CONV_B_WIDTH - 1):])
            cas_l.append(a_s[:, HALO_A - (CONV_A_WIDTH - 1):])
            cbs_l.append(b_s[:, HALO_B - (CONV_B_WIDTH - 1):])
            hp = _moe_call(hp.reshape(bsz * s, D_MODEL), layer, *moe_w, tm=512).reshape(bsz, s, D_MODEL)
            hs = _moe_call(hs.reshape(db * t, D_MODEL), layer, *moe_w, tm=db * t).reshape(db, t, D_MODEL)
        else:
            q, k, v = _qkv_call(hp.reshape(bsz * s, D_MODEL), layer, j, norm_mix_g, w_qkv_b, qg, kg,
                                cos_p, sin_p, t=256, planes=pp)
            att = _attn_p_call(q.reshape(bsz, s, D_MODEL), k.reshape(bsz, s, D_MODEL), v.reshape(bsz, s, D_MODEL),
                               F32 if pp == 2 else BF16)
            kp_l.append(k.reshape(-1, PAGE_SIZE, N_HEADS, HEAD_DIM))
            vp_l.append(v.reshape(-1, PAGE_SIZE, N_HEADS, HEAD_DIM))
            hp = _moe_call(hp.reshape(bsz * s, D_MODEL), layer, *moe_w, att=att.reshape(bsz * s, D_MODEL),
                           w_o=w_o_b, j=j, planes=pp, tm=512).reshape(bsz, s, D_MODEL)
            q, k, v = _qkv_call(hs.reshape(db * t, D_MODEL), layer, j, norm_mix_g, w_qkv_b, qg, kg,
                                cos_s, sin_s, t=db * t, planes=1)
            att = _attn_s_call(page_table, q.reshape(db, t, D_MODEL), k.reshape(db, t, D_MODEL),
                               v.reshape(db, t, D_MODEL), ck, cv, j)
            ks_l.append(k.reshape(db, t, N_HEADS, HEAD_DIM))
            vs_l.append(v.reshape(db, t, N_HEADS, HEAD_DIM))
            hs = _moe_call(hs.reshape(db * t, D_MODEL), layer, *moe_w, att=att.reshape(db * t, D_MODEL),
                           w_o=w_o_b, j=j, tm=db * t).reshape(db, t, D_MODEL)
    return (hp, hs, jnp.stack(kp_l), jnp.stack(vp_l), jnp.stack(ks_l), jnp.stack(vs_l),
            jnp.stack(cap_l), jnp.stack(cbp_l), jnp.stack(cas_l), jnp.stack(cbs_l))
```

```python
import functools

import jax
import jax.numpy as jnp
from jax import lax
from jax.experimental import pallas as pl
from jax.experimental.pallas import tpu as pltpu

F32 = jnp.float32
BF16 = jnp.bfloat16

D_MODEL = 1024
N_HEADS = 16
HEAD_DIM = 64
N_PAIRS = N_HEADS // 2
ROPE_THETA = 10000.0
MOBA_BLOCK = 256
MOBA_TOPK = 3
PAGE_SIZE = 128
D_CONV = 512
CONV_A_WIDTH = 31
CONV_B_WIDTH = 3
HALO_A = 32
HALO_B = 8
N_GROUPS = 4
EXPERTS_PER_GROUP = 4
N_EXPERTS = 16
D_EXPERT = 256
NORM_EPS = 1e-6
NEG_INF = float("-inf")
LANES = 128
VMEM_LIMIT = 56 * 1024 * 1024
GROUP_PAGES = 4
GROUPS_IN_FLIGHT = 3
HI_PRECISION_LAYERS = 2


def _dot(a, b):
    return jnp.dot(a, b, preferred_element_type=F32)


def _dot_nt(a, b):
    return lax.dot_general(a, b, (((1,), (1,)), ((), ())), preferred_element_type=F32)


def _mm(x, w_ref, cols=slice(None)):
    x_hi = x.astype(BF16)
    y = _dot(x_hi, w_ref[0, :, cols])
    if w_ref.shape[0] == 2:
        x_lo = (x - x_hi.astype(F32)).astype(BF16)
        y = y + _dot(x_lo, w_ref[0, :, cols]) + _dot(x_hi, w_ref[1, :, cols])
    return y


def _planes(w, hi):
    w_hi = w.astype(BF16)
    if not hi:
        return w_hi[..., None, :, :]
    w_lo = (w - w_hi.astype(F32)).astype(BF16)
    return jnp.stack([w_hi, w_lo], axis=-3)


def _rms(x, g):
    return x * lax.rsqrt(jnp.mean(x * x, axis=-1, keepdims=True) + NORM_EPS) * g


def _sigmoid(x):
    return 1.0 / (1.0 + jnp.exp(-x))


def _params(*sem):
    return pltpu.CompilerParams(dimension_semantics=sem, vmem_limit_bytes=VMEM_LIMIT)


def _ab_kernel(h_ref, pa_ref, pb_ref, g_ref, win_ref, caw_ref, cab_ref, lng_ref, lnb_ref, cbw_ref, wout_ref,
               o_ref, na_ref, nb_ref, sa_ref, sb_ref, ab_ref, *, bb, t, r, n_tiles):
    i = pl.program_id(1)
    n = bb * t

    @pl.when(i == 0)
    def _():
        sa_ref[:, 0:HALO_A, :] = pa_ref[...]
        sb_ref[:, 0:HALO_B, :] = pb_ref[...]

    x = h_ref[...].reshape(n, D_MODEL)
    xn = _rms(x, g_ref[...])
    a = _mm(xn, win_ref, slice(0, 512)) * _sigmoid(_mm(xn, win_ref, slice(512, 1024)))
    sa_ref[:, HALO_A:HALO_A + t, :] = a.reshape(bb, t, D_CONV)
    bg = _mm(xn, win_ref, slice(1024, 1536))
    ch = _mm(xn, win_ref, slice(1536, 2048)) * _mm(xn, win_ref, slice(2048, 2560))
    sb_ref[:, HALO_B:HALO_B + t, :] = ch.reshape(bb, t, D_CONV)

    off_b = HALO_B - (CONV_B_WIDTH - 1)
    cb = cbw_ref[0:1, :] * sb_ref[:, off_b:off_b + t, :]
    for k in range(1, CONV_B_WIDTH):
        cb = cb + cbw_ref[k:k + 1, :] * sb_ref[:, off_b + k:off_b + k + t, :]
    ab_ref[:, D_CONV:2 * D_CONV] = bg * cb.reshape(n, D_CONV)

    off_a = HALO_A - (CONV_A_WIDTH - 1)
    for b in range(bb):
        for c in range(t // r):
            r0 = c * r + off_a
            acc = caw_ref[0:1, :] * sa_ref[b, r0:r0 + r, :]
            for k in range(1, CONV_A_WIDTH):
                acc = acc + caw_ref[k:k + 1, :] * sa_ref[b, r0 + k:r0 + k + r, :]
            acc = acc + cab_ref[...]
            mu = jnp.mean(acc, axis=-1, keepdims=True)
            xc = acc - mu
            y = xc * lax.rsqrt(jnp.mean(xc * xc, axis=-1, keepdims=True) + NORM_EPS) * lng_ref[...] + lnb_ref[...]
            ab_ref[b * t + c * r:b * t + c * r + r, 0:D_CONV] = y * _sigmoid(y)

    y = _mm(ab_ref[...], wout_ref)
    o_ref[...] = (x + y).reshape(bb, t, D_MODEL)

    na_ref[...] = sa_ref[:, t:t + HALO_A, :]
    nb_ref[...] = sb_ref[:, t:t + HALO_B, :]
    if n_tiles > 1:
        sa_ref[:, 0:HALO_A, :] = sa_ref[:, t:t + HALO_A, :]
        sb_ref[:, 0:HALO_B, :] = sb_ref[:, t:t + HALO_B, :]


def _ab_call(h, prev_a, prev_b, layer, j, norm_g, w_in, ca_w, ca_b, ln_g, ln_b, cb_w, w_out, *, bb, t, planes):
    bsz, s, _ = h.shape
    n_tiles = s // t
    r = min(32, t)
    d_in = w_in.shape[-1]
    kern = functools.partial(_ab_kernel, bb=bb, t=t, r=r, n_tiles=n_tiles)
    return pl.pallas_call(
        kern,
        grid=(bsz // bb, n_tiles),
        in_specs=[
            pl.BlockSpec((bb, t, D_MODEL), lambda b, i: (b, i, 0)),
            pl.BlockSpec((bb, HALO_A, D_CONV), lambda b, i: (b, 0, 0)),
            pl.BlockSpec((bb, HALO_B, D_CONV), lambda b, i: (b, 0, 0)),
            pl.BlockSpec((None, 1, D_MODEL), lambda b, i: (layer, 0, 0)),
            pl.BlockSpec((None, planes, D_MODEL, d_in), lambda b, i: (j, 0, 0, 0)),
            pl.BlockSpec((None, CONV_A_WIDTH, D_CONV), lambda b, i: (j, 0, 0)),
            pl.BlockSpec((None, 1, D_CONV), lambda b, i: (j, 0, 0)),
            pl.BlockSpec((None, 1, D_CONV), lambda b, i: (j, 0, 0)),
            pl.BlockSpec((None, 1, D_CONV), lambda b, i: (j, 0, 0)),
            pl.BlockSpec((None, CONV_B_WIDTH, D_CONV), lambda b, i: (j, 0, 0)),
            pl.BlockSpec((None, planes, 2 * D_CONV, D_MODEL), lambda b, i: (j, 0, 0, 0)),
        ],
        out_specs=[
            pl.BlockSpec((bb, t, D_MODEL), lambda b, i: (b, i, 0)),
            pl.BlockSpec((bb, HALO_A, D_CONV), lambda b, i: (b, 0, 0)),
            pl.BlockSpec((bb, HALO_B, D_CONV), lambda b, i: (b, 0, 0)),
        ],
        out_shape=[
            jax.ShapeDtypeStruct(h.shape, F32),
            jax.ShapeDtypeStruct((bsz, HALO_A, D_CONV), F32),
            jax.ShapeDtypeStruct((bsz, HALO_B, D_CONV), F32),
        ],
        scratch_shapes=[
            pltpu.VMEM((bb, t + HALO_A, D_CONV), F32),
            pltpu.VMEM((bb, t + HALO_B, D_CONV), F32),
            pltpu.VMEM((bb * t, 2 * D_CONV), F32),
        ],
        compiler_params=_params("arbitrary", "arbitrary"),
        name="ab_mixers",
    )(h, prev_a, prev_b, norm_g, w_in, ca_w, ca_b, ln_g, ln_b, cb_w, w_out)


def _qkv_kernel(h_ref, g_ref, w_ref, qg_ref, kg_ref, cos_ref, sin_ref, q_ref, k_ref, v_ref, *, t):
    xn = _rms(h_ref[...], g_ref[...])
    lane = lax.broadcasted_iota(jnp.int32, (t, LANES), 1)
    lo = lane < HEAD_DIM
    first = (lane & (HEAD_DIM - 1)) < HEAD_DIM // 2
    cos = cos_ref[...]
    sin = sin_ref[...]

    def norm_rope(col, gg):
        sq = col * col
        s0 = jnp.sum(jnp.where(lo, sq, 0.0), axis=-1, keepdims=True)
        s1 = jnp.sum(jnp.where(lo, 0.0, sq), axis=-1, keepdims=True)
        ms = jnp.where(lo, s0, s1) * (1.0 / HEAD_DIM)
        y = col * lax.rsqrt(ms + NORM_EPS) * gg
        rot = jnp.where(first, pltpu.roll(y, LANES - HEAD_DIM // 2, 1), pltpu.roll(y, HEAD_DIM // 2, 1))
        return y * cos + rot * sin

    qa = _mm(xn, w_ref, slice(0, D_MODEL))
    for c in range(N_PAIRS):
        q_ref[:, c * LANES:(c + 1) * LANES] = norm_rope(qa[:, c * LANES:(c + 1) * LANES], qg_ref[...])
    ka = _mm(xn, w_ref, slice(D_MODEL, 2 * D_MODEL))
    for c in range(N_PAIRS):
        k_ref[:, c * LANES:(c + 1) * LANES] = norm_rope(ka[:, c * LANES:(c + 1) * LANES], kg_ref[...])
    v_ref[...] = _mm(xn, w_ref, slice(2 * D_MODEL, 3 * D_MODEL))


def _qkv_call(h2, layer, j, norm_g, w_qkv, qg, kg, cos_t, sin_t, *, t, planes):
    n = h2.shape[0]
    n_pos_tiles = cos_t.shape[0] // t
    kern = functools.partial(_qkv_kernel, t=t)
    tok = pl.BlockSpec((t, D_MODEL), lambda i: (i, 0))
    tab = pl.BlockSpec((t, LANES), lambda i: (i % n_pos_tiles, 0))
    return pl.pallas_call(
        kern,
        grid=(n // t,),
        in_specs=[
            tok,
            pl.BlockSpec((None, 1, D_MODEL), lambda i: (layer, 0, 0)),
            pl.BlockSpec((None, planes, D_MODEL, 3 * D_MODEL), lambda i: (j, 0, 0, 0)),
            pl.BlockSpec((None, 1, LANES), lambda i: (j, 0, 0)),
            pl.BlockSpec((None, 1, LANES), lambda i: (j, 0, 0)),
            tab, tab,
        ],
        out_specs=[tok, tok, tok],
        out_shape=[jax.ShapeDtypeStruct((n, D_MODEL), F32)] * 3,
        compiler_params=_params("arbitrary"),
        name="qkv_rope",
    )(h2, norm_g, w_qkv, qg, kg, cos_t, sin_t)


def _top_blocks(gate, lanef):
    sel = jnp.zeros(gate.shape, F32)
    for _ in range(MOBA_TOPK):
        mx = jnp.max(gate, axis=-1, keepdims=True)
        idx = jnp.min(jnp.where(gate == mx, lanef, 1e9), axis=-1, keepdims=True)
        pick = (lanef == idx) & (mx > NEG_INF)
        sel = jnp.where(pick, 1.0, sel)
        gate = jnp.where(pick, NEG_INF, gate)
    return sel


def _attn_p_tile(c, q_ref, o_ref, kb_s, vb_s, km_s):
    blk = MOBA_BLOCK
    q = q_ref[0] * (HEAD_DIM ** -0.5)
    lane = lax.broadcasted_iota(jnp.int32, (blk, LANES), 1)
    lanef = lane.astype(F32)
    causal = (lax.broadcasted_iota(jnp.int32, (blk, blk), 1) <= lax.broadcasted_iota(jnp.int32, (blk, blk), 0))
    outs = []
    for j in range(2):
        qjf = jnp.where((lane < HEAD_DIM) if j == 0 else (lane >= HEAD_DIM), q, 0.0)
        qj = qjf.astype(BF16)
        blocks = [c] + list(range(c))
        scores = [jnp.where(causal, _dot_nt(qj, kb_s[c * blk:(c + 1) * blk, :]), NEG_INF)]
        if c > MOBA_TOPK:
            q_lo = (qjf - qj.astype(F32)).astype(BF16)
            gate = _dot_nt(qj, km_s[0]) + _dot_nt(q_lo, km_s[0]) + _dot_nt(qj, km_s[1])
            sel = _top_blocks(jnp.where(lane < c, gate, NEG_INF), lanef)
        for n in range(c):
            sn = _dot_nt(qj, kb_s[n * blk:(n + 1) * blk, :])
            if c > MOBA_TOPK:
                col = jnp.max(jnp.where(lane == n, sel, 0.0), axis=-1, keepdims=True)
                sn = jnp.where(col > 0.0, sn, NEG_INF)
            scores.append(sn)
        mx = scores[0]
        for sn in scores[1:]:
            mx = jnp.maximum(mx, sn)
        m = jnp.max(mx, axis=-1, keepdims=True)
        l_el = None
        acc = None
        for n, sn in zip(blocks, scores):
            p = jnp.exp(sn - m)
            pv = _dot(p.astype(BF16), vb_s[n * blk:(n + 1) * blk, :])
            l_el = p if l_el is None else l_el + p
            acc = pv if acc is None else acc + pv
        outs.append(acc / jnp.sum(l_el, axis=-1, keepdims=True))
    o_ref[0] = jnp.where(lane < HEAD_DIM, outs[0], outs[1]).astype(o_ref.dtype)


def _attn_p_kernel(q_ref, k_ref, v_ref, o_ref, kb_s, vb_s, km_s, km32_s, *, n_blk):
    i = pl.program_id(2)
    blk = MOBA_BLOCK

    @pl.when(i == 0)
    def _():
        kb_s[...] = k_ref[0].astype(BF16)
        vb_s[...] = v_ref[0].astype(BF16)
        km32_s[...] = jnp.zeros(km32_s.shape, F32)
        for n in range(n_blk):
            km32_s[n:n + 1, :] = jnp.mean(k_ref[0, n * blk:(n + 1) * blk, :], axis=0, keepdims=True)
        km = km32_s[...]
        km_hi = km.astype(BF16)
        km_s[0] = km_hi
        km_s[1] = (km - km_hi.astype(F32)).astype(BF16)

    for c in range(n_blk):
        @pl.when(i == c)
        def _(c=c):
            _attn_p_tile(c, q_ref, o_ref, kb_s, vb_s, km_s)


def _attn_p_call(q, k, v, out_dtype):
    bsz, s, _ = q.shape
    n_blk = s // MOBA_BLOCK
    assert n_blk <= LANES
    kern = functools.partial(_attn_p_kernel, n_blk=n_blk)
    qs = pl.BlockSpec((1, MOBA_BLOCK, LANES), lambda b, p, i: (b, i, p))
    kvs = pl.BlockSpec((1, s, LANES), lambda b, p, i: (b, 0, p))
    return pl.pallas_call(
        kern,
        grid=(bsz, N_PAIRS, n_blk),
        in_specs=[qs, kvs, kvs],
        out_specs=qs,
        out_shape=jax.ShapeDtypeStruct((bsz, s, D_MODEL), out_dtype),
        scratch_shapes=[
            pltpu.VMEM((s, LANES), BF16),
            pltpu.VMEM((s, LANES), BF16),
            pltpu.VMEM((2, LANES, LANES), BF16),
            pltpu.VMEM((LANES, LANES), F32),
        ],
        compiler_params=_params("arbitrary", "arbitrary", "arbitrary"),
        name="moba_prompt",
    )(q, k, v)


def _attn_s_kernel(pt_ref, q_ref, kn_ref, vn_ref, ck_hbm, cv_hbm, o_ref,
                   kbuf, vbuf, ksem, vsem, qbd_s, m_s, l_s, g_s, acc_s, sc_s, pr_s, *, layer, n_pages, t):
    b = pl.program_id(0)
    rows = 2 * t
    ppb = MOBA_BLOCK // PAGE_SIZE
    n_blocks = n_pages // ppb
    n_groups = n_pages // GROUP_PAGES
    n_slots = kbuf.shape[0]

    def copies(pg):
        slot = pg % n_slots
        src = pt_ref[b, pg]
        return (pltpu.make_async_copy(ck_hbm.at[layer, src], kbuf.at[slot], ksem.at[slot]),
                pltpu.make_async_copy(cv_hbm.at[layer, src], vbuf.at[slot], vsem.at[slot]))

    def start_group(g):
        for pg in range(g * GROUP_PAGES, (g + 1) * GROUP_PAGES):
            for cp in copies(pg):
                cp.start()

    def wait_group(g):
        for pg in range(g * GROUP_PAGES, (g + 1) * GROUP_PAGES):
            for cp in copies(pg):
                cp.wait()

    for g in range(min(GROUPS_IN_FLIGHT, n_groups)):
        start_group(g)

    lane = lax.broadcasted_iota(jnp.int32, (rows, LANES), 1)
    row = lax.broadcasted_iota(jnp.int32, (rows, LANES), 0)
    lanef = lane.astype(F32)
    head_mask = (lane < HEAD_DIM) == (row < t)
    q = q_ref[0] * (HEAD_DIM ** -0.5)
    qf = []
    for hp in range(N_PAIRS):
        qc = q[:, hp * LANES:(hp + 1) * LANES]
        qf.append(jnp.where(head_mask, jnp.concatenate([qc, qc], axis=0), 0.0))
        qbd_s[hp] = qf[hp].astype(BF16)
    m_s[...] = jnp.full(m_s.shape, NEG_INF, F32)
    l_s[...] = jnp.zeros(l_s.shape, F32)
    g_s[...] = jnp.zeros(g_s.shape, F32)

    bpg = GROUP_PAGES // ppb
    units = [(hp, bi) for hp in range(N_PAIRS) for bi in range(bpg)]
    for g in range(n_groups):
        wait_group(g)

        def unit_slots(bi):
            return [(g * GROUP_PAGES + bi * ppb + u) % n_slots for u in range(ppb)]

        for ui, (hp, bi) in enumerate(units):
            hs = slice(hp * LANES, (hp + 1) * LANES)
            for u, sl in enumerate(unit_slots(bi)):
                sc_s[ui, :, u * PAGE_SIZE:(u + 1) * PAGE_SIZE] = _dot(qbd_s[hp], kbuf[sl, hs, :].astype(BF16))
        for ui, (hp, bi) in enumerate(units):
            sc = sc_s[ui]
            m = jnp.max(sc, axis=-1, keepdims=True)
            p = jnp.exp(sc - m)
            pr_s[ui] = p.astype(BF16)
            here = lane == g * bpg + bi
            m_s[hp] = jnp.where(here, m, m_s[hp])
            l_s[hp] = jnp.where(here, jnp.sum(p, axis=-1, keepdims=True), l_s[hp])
            g_s[hp] = jnp.where(here, jnp.sum(sc, axis=-1, keepdims=True), g_s[hp])
        for ui, (hp, bi) in enumerate(units):
            hs = slice(hp * LANES, (hp + 1) * LANES)
            acc = None
            for u, sl in enumerate(unit_slots(bi)):
                pv = _dot_nt(pr_s[ui, :, u * PAGE_SIZE:(u + 1) * PAGE_SIZE], vbuf[sl, hs, :].astype(BF16))
                acc = pv if acc is None else acc + pv
            acc_s[g * bpg + bi, hp] = acc
        if g + GROUPS_IN_FLIGHT < n_groups:
            start_group(g + GROUPS_IN_FLIGHT)

    kn = kn_ref[0]
    vn = vn_ref[0]
    t_row = (row[:, 0:1] & (t - 1))
    cols = []
    for hp in range(N_PAIRS):
        kc = kn[:, hp * LANES:(hp + 1) * LANES]
        vc = vn[:, hp * LANES:(hp + 1) * LANES]
        s_own = []
        for t2 in range(t):
            st = jnp.sum(qf[hp] * kc[t2:t2 + 1, :], axis=-1, keepdims=True)
            s_own.append(jnp.where(t_row >= t2, st, NEG_INF))
        m_own = s_own[0]
        for t2 in range(1, t):
            m_own = jnp.maximum(m_own, s_own[t2])
        l_own = jnp.zeros((rows, 1), F32)
        acc_own = jnp.zeros((rows, LANES), F32)
        for t2 in range(t):
            pt2 = jnp.exp(s_own[t2] - m_own)
            l_own = l_own + pt2
            acc_own = acc_own + pt2 * vc[t2:t2 + 1, :]
        sel = _top_blocks(jnp.where(lane < n_blocks, g_s[hp], NEG_INF), lanef) > 0.0
        mm = m_s[hp]
        big = jnp.maximum(jnp.max(jnp.where(sel, mm, NEG_INF), axis=-1, keepdims=True), m_own)
        w = jnp.where(sel, jnp.exp(mm - big), 0.0)
        w_own = jnp.exp(m_own - big)
        den = jnp.sum(w * l_s[hp], axis=-1, keepdims=True) + w_own * l_own
        num = w_own * acc_own
        for n in range(n_blocks):
            num = num + w[:, n:n + 1] * acc_s[n, hp]
        o = num / den
        cols.append(jnp.where(lane[0:t] < HEAD_DIM, o[0:t], o[t:rows]))
    o_ref[0] = jnp.concatenate(cols, axis=1)


def _attn_s_call(page_table, q, k_new, v_new, ck, cv, layer):
    db, t, _ = q.shape
    n_pages = page_table.shape[1]
    n_blocks = n_pages * PAGE_SIZE // MOBA_BLOCK
    assert (n_pages * PAGE_SIZE) % MOBA_BLOCK == 0 and n_blocks <= LANES
    assert GROUP_PAGES % (MOBA_BLOCK // PAGE_SIZE) == 0 and n_pages % GROUP_PAGES == 0
    assert t & (t - 1) == 0
    kern = functools.partial(_attn_s_kernel, layer=layer, n_pages=n_pages, t=t)
    tok = pl.BlockSpec((1, t, D_MODEL), lambda b, pt: (b, 0, 0))
    rows = 2 * t
    n_slots = GROUPS_IN_FLIGHT * GROUP_PAGES
    n_units = N_PAIRS * GROUP_PAGES * PAGE_SIZE // MOBA_BLOCK
    grid_spec = pltpu.PrefetchScalarGridSpec(
        num_scalar_prefetch=1,
        grid=(db,),
        in_specs=[tok, tok, tok, pl.BlockSpec(memory_space=pl.ANY), pl.BlockSpec(memory_space=pl.ANY)],
        out_specs=tok,
        scratch_shapes=[
            pltpu.VMEM((n_slots, D_MODEL, PAGE_SIZE), F32),
            pltpu.VMEM((n_slots, D_MODEL, PAGE_SIZE), F32),
            pltpu.SemaphoreType.DMA((n_slots,)),
            pltpu.SemaphoreType.DMA((n_slots,)),
            pltpu.VMEM((N_PAIRS, rows, LANES), BF16),
            pltpu.VMEM((N_PAIRS, rows, LANES), F32),
            pltpu.VMEM((N_PAIRS, rows, LANES), F32),
            pltpu.VMEM((N_PAIRS, rows, LANES), F32),
            pltpu.VMEM((n_blocks, N_PAIRS, rows, LANES), F32),
            pltpu.VMEM((n_units, rows, MOBA_BLOCK), F32),
            pltpu.VMEM((n_units, rows, MOBA_BLOCK), BF16),
        ],
    )
    return pl.pallas_call(
        kern,
        grid_spec=grid_spec,
        out_shape=jax.ShapeDtypeStruct((db, t, D_MODEL), F32),
        compiler_params=_params("arbitrary"),
        name="moba_sample",
    )(page_table, q, k_new, v_new, ck, cv)


def _route(logits):
    lane = lax.broadcasted_iota(jnp.int32, logits.shape, 1)
    lanef = lane.astype(F32)
    is_g = (lane >= N_EXPERTS) & (lane < N_EXPERTS + N_GROUPS)
    lg = jnp.where(is_g, logits, NEG_INF)
    mg = jnp.max(lg, axis=-1, keepdims=True)
    grp = jnp.min(jnp.where(lg == mg, lanef, 1e9), axis=-1, keepdims=True) - float(N_EXPERTS)
    p_grp = 1.0 / jnp.sum(jnp.exp(lg - mg), axis=-1, keepdims=True)
    member = (lane < N_EXPERTS) & (lax.shift_right_logical(lane, 2).astype(F32) == grp)
    le = jnp.where(member, logits, NEG_INF)
    m1 = jnp.max(le, axis=-1, keepdims=True)
    i1 = jnp.min(jnp.where(le == m1, lanef, 1e9), axis=-1, keepdims=True)
    le2 = jnp.where(lanef == i1, NEG_INF, le)
    m2 = jnp.max(le2, axis=-1, keepdims=True)
    i2 = jnp.min(jnp.where(le2 == m2, lanef, 1e9), axis=-1, keepdims=True)
    e2 = jnp.exp(m2 - m1)
    den = 1.0 + e2
    return jnp.where(lanef == i1, p_grp / den, 0.0) + jnp.where(lanef == i2, p_grp * e2 / den, 0.0)


def _moe_kernel(*refs, has_proj, tm):
    if has_proj:
        h_ref, att_ref, wo_ref = refs[:3]
        refs = refs[3:]
    else:
        h_ref = refs[0]
        refs = refs[1:]
    g_ref, wr_ref, br_ref, wg_ref, wu_ref, wd_ref, o_ref, xn_s, comb_s, acc_s = refs
    gi = pl.program_id(1)

    @pl.when(gi == 0)
    def _():
        hm = h_ref[...]
        if has_proj:
            hm = hm + _mm(att_ref[...].astype(F32), wo_ref)
        acc_s[...] = hm
        xn = _rms(hm, g_ref[...])
        xn_s[...] = xn.astype(BF16)
        logits = jnp.dot(xn, wr_ref[...], precision=lax.Precision.HIGHEST, preferred_element_type=F32)
        comb_s[...] = _route(logits + br_ref[...])

    x = xn_s[...]
    lane = lax.broadcasted_iota(jnp.int32, (tm, LANES), 1)
    comb = comb_s[...]
    for e in range(EXPERTS_PER_GROUP):
        c = jnp.sum(jnp.where(lane == gi * EXPERTS_PER_GROUP + e, comb, 0.0), axis=-1, keepdims=True)
        hg = _dot(x, wg_ref[e])
        hh = hg * _sigmoid(hg) * _dot(x, wu_ref[e]) * c
        acc_s[...] += _dot(hh.astype(BF16), wd_ref[e])

    @pl.when(gi == N_GROUPS - 1)
    def _():
        o_ref[...] = acc_s[...]


def _moe_call(h2, layer, norm_g, wr, br, wg, wu, wd, att=None, w_o=None, j=0, planes=1, *, tm):
    n = h2.shape[0]
    has_proj = att is not None
    kern = functools.partial(_moe_kernel, has_proj=has_proj, tm=tm)
    tok = pl.BlockSpec((tm, D_MODEL), lambda i, g: (i, 0))
    in_specs = [tok]
    args = [h2]
    if has_proj:
        in_specs += [tok, pl.BlockSpec((None, planes, D_MODEL, D_MODEL), lambda i, g: (j, 0, 0, 0))]
        args += [att, w_o]
    in_specs += [
        pl.BlockSpec((None, 1, D_MODEL), lambda i, g: (layer, 0, 0)),
        pl.BlockSpec((None, D_MODEL, LANES), lambda i, g: (layer, 0, 0)),
        pl.BlockSpec((None, 1, LANES), lambda i, g: (layer, 0, 0)),
        pl.BlockSpec((None, None, EXPERTS_PER_GROUP, D_MODEL, D_EXPERT), lambda i, g: (layer, g, 0, 0, 0)),
        pl.BlockSpec((None, None, EXPERTS_PER_GROUP, D_MODEL, D_EXPERT), lambda i, g: (layer, g, 0, 0, 0)),
        pl.BlockSpec((None, None, EXPERTS_PER_GROUP, D_EXPERT, D_MODEL), lambda i, g: (layer, g, 0, 0, 0)),
    ]
    args += [norm_g, wr, br, wg, wu, wd]
    return pl.pallas_call(
        kern,
        grid=(n // tm, N_GROUPS),
        in_specs=in_specs,
        out_specs=tok,
        out_shape=jax.ShapeDtypeStruct((n, D_MODEL), F32),
        scratch_shapes=[
            pltpu.VMEM((tm, D_MODEL), BF16),
            pltpu.VMEM((tm, LANES), F32),
            pltpu.VMEM((tm, D_MODEL), F32),
        ],
        compiler_params=_params("arbitrary", "arbitrary"),
        name="moe_proj" if has_proj else "moe",
    )(*args)


def _rope_tables(pos):
    half = HEAD_DIM // 2
    inv = ROPE_THETA ** (-(jnp.arange(half, dtype=F32) * 2.0 / HEAD_DIM))
    ang = pos.astype(F32)[:, None] * inv[None, :]
    cos = jnp.cos(ang)
    sin = jnp.sin(ang)
    cos_h = jnp.concatenate([cos, cos], axis=-1)
    sin_h = jnp.concatenate([-sin, sin], axis=-1)
    return jnp.concatenate([cos_h, cos_h], axis=-1), jnp.concatenate([sin_h, sin_h], axis=-1)


def kernel(x_prompt, x_sample, cache_k, cache_v, page_table, state_conv_a, state_conv_b, norm_mix_g, norm_ffn_g,
           w_in_ab, conv_a_w, conv_a_b, ln_a_g, ln_a_b, conv_b_w, w_out_ab, w_qkv, q_norm_g, k_norm_g, w_o,
           router_g_w, router_g_b, router_e_w, router_e_b, w_gate, w_up, w_down):
    bsz, s, _ = x_prompt.shape
    db, t, _ = x_sample.shape
    depth = norm_mix_g.shape[0]
    n_pages = page_table.shape[1]
    past = n_pages * PAGE_SIZE
    n_pool = cache_k.shape[1]

    w_in_b = _planes(w_in_ab, True)
    w_out_b = _planes(w_out_ab, True)
    w_qkv_b = _planes(w_qkv, True)
    w_o_b = _planes(w_o, True)
    wg_b = w_gate.astype(BF16).reshape(depth, N_GROUPS, EXPERTS_PER_GROUP, D_MODEL, D_EXPERT)
    wu_b = w_up.astype(BF16).reshape(depth, N_GROUPS, EXPERTS_PER_GROUP, D_MODEL, D_EXPERT)
    wd_b = w_down.astype(BF16).reshape(depth, N_GROUPS, EXPERTS_PER_GROUP, D_EXPERT, D_MODEL)
    pad = LANES - N_EXPERTS - N_GROUPS
    wr = jnp.concatenate([router_e_w, router_g_w, jnp.zeros((depth, D_MODEL, pad), F32)], axis=-1)
    br = jnp.concatenate([router_e_b, router_g_b, jnp.zeros((depth, pad), F32)], axis=-1)[:, None, :]
    qg = jnp.concatenate([q_norm_g, q_norm_g], axis=-1)[:, None, :]
    kg = jnp.concatenate([k_norm_g, k_norm_g], axis=-1)[:, None, :]
    norm_mix_g = norm_mix_g[:, None, :]
    norm_ffn_g = norm_ffn_g[:, None, :]
    conv_a_b = conv_a_b[:, None, :]
    ln_a_g = ln_a_g[:, None, :]
    ln_a_b = ln_a_b[:, None, :]
    cos_p, sin_p = _rope_tables(jnp.arange(s, dtype=jnp.int32))
    cos_s, sin_s = _rope_tables(past + jnp.arange(t, dtype=jnp.int32))
    cos_s = jnp.tile(cos_s, (db, 1))
    sin_s = jnp.tile(sin_s, (db, 1))
    ck = jnp.transpose(cache_k, (0, 1, 3, 4, 2)).reshape(cache_k.shape[0], n_pool, D_MODEL, PAGE_SIZE)
    cv = jnp.transpose(cache_v, (0, 1, 3, 4, 2)).reshape(cache_v.shape[0], n_pool, D_MODEL, PAGE_SIZE)

    hp, hs = x_prompt, x_sample
    kp_l, vp_l, ks_l, vs_l = [], [], [], []
    cap_l, cbp_l, cas_l, cbs_l = [], [], [], []
    moe_w = (norm_ffn_g, wr, br, wg_b, wu_b, wd_b)
    for layer in range(depth):
        j = layer // 2
        pp = 2 if layer < HI_PRECISION_LAYERS else 1
        if layer % 2 == 0:
            ab_w = (norm_mix_g, w_in_b, conv_a_w, conv_a_b, ln_a_g, ln_a_b, conv_b_w, w_out_b)
            zero_a = jnp.zeros((bsz, HALO_A, D_CONV), F32)
            zero_b = jnp.zeros((bsz, HALO_B, D_CONV), F32)
            hp, a_p, b_p = _ab_call(hp, zero_a, zero_b, layer, j, *ab_w, bb=1, t=256, planes=pp)
            prev_a = jnp.pad(state_conv_a[j], ((0, 0), (HALO_A - (CONV_A_WIDTH - 1), 0), (0, 0)))
            prev_b = jnp.pad(state_conv_b[j], ((0, 0), (HALO_B - (CONV_B_WIDTH - 1), 0), (0, 0)))
            hs, a_s, b_s = _ab_call(hs, prev_a, prev_b, layer, j, *ab_w, bb=db, t=t, planes=1)
            cap_l.append(a_p[:, HALO_A - (CONV_A_WIDTH - 1):])
            cbp_l.append(b_p[:, HALO_B - (CONV_B_WIDTH - 1):])
            cas_l.append(a_s[:, HALO_A - (CONV_A_WIDTH - 1):])
            cbs_l.append(b_s[:, HALO_B - (CONV_B_WIDTH - 1):])
            hp = _moe_call(hp.reshape(bsz * s, D_MODEL), layer, *moe_w, tm=512).reshape(bsz, s, D_MODEL)
            hs = _moe_call(hs.reshape(db * t, D_MODEL), layer, *moe_w, tm=db * t).reshape(db, t, D_MODEL)
        else:
            q, k, v = _qkv_call(hp.reshape(bsz * s, D_MODEL), layer, j, norm_mix_g, w_qkv_b, qg, kg,
                                cos_p, sin_p, t=256, planes=pp)
            att = _attn_p_call(q.reshape(bsz, s, D_MODEL), k.reshape(bsz, s, D_MODEL), v.reshape(bsz, s, D_MODEL),
                               F32 if pp == 2 else BF16)
            kp_l.append(k.reshape(-1, PAGE_SIZE, N_HEADS, HEAD_DIM))
            vp_l.append(v.reshape(-1, PAGE_SIZE, N_HEADS, HEAD_DIM))
            hp = _moe_call(hp.reshape(bsz * s, D_MODEL), layer, *moe_w, att=att.reshape(bsz * s, D_MODEL),
                           w_o=w_o_b, j=j, planes=pp, tm=512).reshape(bsz, s, D_MODEL)
            q, k, v = _qkv_call(hs.reshape(db * t, D_MODEL), layer, j, norm_mix_g, w_qkv_b, qg, kg,
                                cos_s, sin_s, t=db * t, planes=1)
            att = _attn_s_call(page_table, q.reshape(db, t, D_MODEL), k.reshape(db, t, D_MODEL),
                               v.reshape(db, t, D_MODEL), ck, cv, j)
            ks_l.append(k.reshape(db, t, N_HEADS, HEAD_DIM))
            vs_l.append(v.reshape(db, t, N_HEADS, HEAD_DIM))
            hs = _moe_call(hs.reshape(db * t, D_MODEL), layer, *moe_w, att=att.reshape(db * t, D_MODEL),
                           w_o=w_o_b, j=j, tm=db * t).reshape(db, t, D_MODEL)
    return (hp, hs, jnp.stack(kp_l), jnp.stack(vp_l), jnp.stack(ks_l), jnp.stack(vs_l),
            jnp.stack(cap_l), jnp.stack(cbp_l), jnp.stack(cas_l), jnp.stack(cbs_l))
```

```python
import functools

import jax
import jax.numpy as jnp
from jax import lax
from jax.experimental import pallas as pl
from jax.experimental.pallas import tpu as pltpu

F32 = jnp.float32
BF16 = jnp.bfloat16

D_MODEL = 1024
N_HEADS = 16
HEAD_DIM = 64
N_PAIRS = N_HEADS // 2
ROPE_THETA = 10000.0
MOBA_BLOCK = 256
MOBA_TOPK = 3
PAGE_SIZE = 128
D_CONV = 512
CONV_A_WIDTH = 31
CONV_B_WIDTH = 3
HALO_A = 32
HALO_B = 8
N_GROUPS = 4
EXPERTS_PER_GROUP = 4
N_EXPERTS = 16
D_EXPERT = 256
NORM_EPS = 1e-6
NEG_INF = float("-inf")
LANES = 128
SUBLANES = 8
VMEM_LIMIT = 56 * 1024 * 1024
GROUP_PAGES = 4
GROUPS_IN_FLIGHT = 3
HI_PRECISION_LAYERS = 2


def _dot(a, b):
    return jnp.dot(a, b, preferred_element_type=F32)


def _dot_nt(a, b):
    return lax.dot_general(a, b, (((1,), (1,)), ((), ())), preferred_element_type=F32)


def _mm(x, w_ref, cols=slice(None)):
    x_hi = x.astype(BF16)
    y = _dot(x_hi, w_ref[0, :, cols])
    if w_ref.shape[0] == 2:
        x_lo = (x - x_hi.astype(F32)).astype(BF16)
        y = y + _dot(x_lo, w_ref[0, :, cols]) + _dot(x_hi, w_ref[1, :, cols])
    return y


def _planes(w, hi):
    w_hi = w.astype(BF16)
    if not hi:
        return w_hi[..., None, :, :]
    w_lo = (w - w_hi.astype(F32)).astype(BF16)
    return jnp.stack([w_hi, w_lo], axis=-3)


def _rms(x, g):
    return x * lax.rsqrt(jnp.mean(x * x, axis=-1, keepdims=True) + NORM_EPS) * g


def _sigmoid(x):
    return 1.0 / (1.0 + jnp.exp(-x))


def _params(*sem):
    return pltpu.CompilerParams(dimension_semantics=sem, vmem_limit_bytes=VMEM_LIMIT)


def _ab_kernel(h_ref, pa_ref, pb_ref, g_ref, win_ref, caw_ref, cab_ref, lng_ref, lnb_ref, cbw_ref, wout_ref,
               o_ref, na_ref, nb_ref, sa_ref, sb_ref, ab_ref, *maybe_sh, bb, t, r, n_tiles):
    i = pl.program_id(1)
    n = bb * t

    @pl.when(i == 0)
    def _():
        sa_ref[:, 0:HALO_A, :] = pa_ref[...]
        sb_ref[:, 0:HALO_B, :] = pb_ref[...]

    x = h_ref[...].reshape(n, D_MODEL)
    xn = _rms(x, g_ref[...])
    a = _mm(xn, win_ref, slice(0, 512)) * _sigmoid(_mm(xn, win_ref, slice(512, 1024)))
    sa_ref[:, HALO_A:HALO_A + t, :] = a.reshape(bb, t, D_CONV)
    bg = _mm(xn, win_ref, slice(1024, 1536))
    ch = _mm(xn, win_ref, slice(1536, 2048)) * _mm(xn, win_ref, slice(2048, 2560))
    sb_ref[:, HALO_B:HALO_B + t, :] = ch.reshape(bb, t, D_CONV)

    off_b = HALO_B - (CONV_B_WIDTH - 1)
    cb = cbw_ref[0:1, :] * sb_ref[:, off_b:off_b + t, :]
    for k in range(1, CONV_B_WIDTH):
        cb = cb + cbw_ref[k:k + 1, :] * sb_ref[:, off_b + k:off_b + k + t, :]
    ab_ref[:, D_CONV:2 * D_CONV] = bg * cb.reshape(n, D_CONV)

    off_a = HALO_A - (CONV_A_WIDTH - 1)
    sh_ref = maybe_sh[0] if maybe_sh else None
    if sh_ref is not None:
        for s in range(1, SUBLANES):
            sh_ref[s - 1] = sa_ref[0, s:s + sh_ref.shape[1], :]
    for b in range(bb):
        for c in range(t // r):
            acc = None
            for k in range(CONV_A_WIDTH):
                kk = k + off_a
                if sh_ref is not None and kk % SUBLANES:
                    base = c * r + kk // SUBLANES * SUBLANES
                    tap = sh_ref[kk % SUBLANES - 1, base:base + r, :]
                else:
                    tap = sa_ref[b, c * r + kk:c * r + kk + r, :]
                term = caw_ref[k:k + 1, :] * tap
                acc = term if acc is None else acc + term
            acc = acc + cab_ref[...]
            mu = jnp.mean(acc, axis=-1, keepdims=True)
            xc = acc - mu
            y = xc * lax.rsqrt(jnp.mean(xc * xc, axis=-1, keepdims=True) + NORM_EPS) * lng_ref[...] + lnb_ref[...]
            ab_ref[b * t + c * r:b * t + c * r + r, 0:D_CONV] = y * _sigmoid(y)

    y = _mm(ab_ref[...], wout_ref)
    o_ref[...] = (x + y).reshape(bb, t, D_MODEL)

    na_ref[...] = sa_ref[:, t:t + HALO_A, :]
    nb_ref[...] = sb_ref[:, t:t + HALO_B, :]
    if n_tiles > 1:
        sa_ref[:, 0:HALO_A, :] = sa_ref[:, t:t + HALO_A, :]
        sb_ref[:, 0:HALO_B, :] = sb_ref[:, t:t + HALO_B, :]


def _ab_call(h, prev_a, prev_b, layer, j, norm_g, w_in, ca_w, ca_b, ln_g, ln_b, cb_w, w_out, *, bb, t, planes):
    bsz, s, _ = h.shape
    n_tiles = s // t
    r = min(32, t)
    d_in = w_in.shape[-1]
    kern = functools.partial(_ab_kernel, bb=bb, t=t, r=r, n_tiles=n_tiles)
    scratch = [
        pltpu.VMEM((bb, t + HALO_A, D_CONV), F32),
        pltpu.VMEM((bb, t + HALO_B, D_CONV), F32),
        pltpu.VMEM((bb * t, 2 * D_CONV), F32),
    ]
    if bb == 1 and t >= HALO_A:
        scratch.append(pltpu.VMEM((SUBLANES - 1, t + HALO_A - SUBLANES, D_CONV), F32))
    return pl.pallas_call(
        kern,
        grid=(bsz // bb, n_tiles),
        in_specs=[
            pl.BlockSpec((bb, t, D_MODEL), lambda b, i: (b, i, 0)),
            pl.BlockSpec((bb, HALO_A, D_CONV), lambda b, i: (b, 0, 0)),
            pl.BlockSpec((bb, HALO_B, D_CONV), lambda b, i: (b, 0, 0)),
            pl.BlockSpec((None, 1, D_MODEL), lambda b, i: (layer, 0, 0)),
            pl.BlockSpec((None, planes, D_MODEL, d_in), lambda b, i: (j, 0, 0, 0)),
            pl.BlockSpec((None, CONV_A_WIDTH, D_CONV), lambda b, i: (j, 0, 0)),
            pl.BlockSpec((None, 1, D_CONV), lambda b, i: (j, 0, 0)),
            pl.BlockSpec((None, 1, D_CONV), lambda b, i: (j, 0, 0)),
            pl.BlockSpec((None, 1, D_CONV), lambda b, i: (j, 0, 0)),
            pl.BlockSpec((None, CONV_B_WIDTH, D_CONV), lambda b, i: (j, 0, 0)),
            pl.BlockSpec((None, planes, 2 * D_CONV, D_MODEL), lambda b, i: (j, 0, 0, 0)),
        ],
        out_specs=[
            pl.BlockSpec((bb, t, D_MODEL), lambda b, i: (b, i, 0)),
            pl.BlockSpec((bb, HALO_A, D_CONV), lambda b, i: (b, 0, 0)),
            pl.BlockSpec((bb, HALO_B, D_CONV), lambda b, i: (b, 0, 0)),
        ],
        out_shape=[
            jax.ShapeDtypeStruct(h.shape, F32),
            jax.ShapeDtypeStruct((bsz, HALO_A, D_CONV), F32),
            jax.ShapeDtypeStruct((bsz, HALO_B, D_CONV), F32),
        ],
        scratch_shapes=scratch,
        compiler_params=_params("arbitrary", "arbitrary"),
        name="ab_mixers",
    )(h, prev_a, prev_b, norm_g, w_in, ca_w, ca_b, ln_g, ln_b, cb_w, w_out)


def _qkv_kernel(*refs, t, paged):
    h_ref, g_ref, w_ref, qg_ref, kg_ref, cos_ref, sin_ref = refs[:7]
    outs = refs[9:] if paged else refs[7:]
    q_ref, k_ref, v_ref = outs[:3]
    kt_ref, vt_ref = outs[3:] if paged else (None, None)
    xn = _rms(h_ref[...], g_ref[...])
    lane = lax.broadcasted_iota(jnp.int32, (t, LANES), 1)
    lo = lane < HEAD_DIM
    first = (lane & (HEAD_DIM - 1)) < HEAD_DIM // 2
    cos = cos_ref[...]
    sin = sin_ref[...]

    def norm_rope(col, gg):
        sq = col * col
        s0 = jnp.sum(jnp.where(lo, sq, 0.0), axis=-1, keepdims=True)
        s1 = jnp.sum(jnp.where(lo, 0.0, sq), axis=-1, keepdims=True)
        ms = jnp.where(lo, s0, s1) * (1.0 / HEAD_DIM)
        y = col * lax.rsqrt(ms + NORM_EPS) * gg
        rot = jnp.where(first, pltpu.roll(y, LANES - HEAD_DIM // 2, 1), pltpu.roll(y, HEAD_DIM // 2, 1))
        return y * cos + rot * sin

    qa = _mm(xn, w_ref, slice(0, D_MODEL))
    for c in range(N_PAIRS):
        q_ref[:, c * LANES:(c + 1) * LANES] = norm_rope(qa[:, c * LANES:(c + 1) * LANES], qg_ref[...])
    def put_paged(dst_ref, col, c):
        for p in range(t // PAGE_SIZE):
            dst_ref[p, c * LANES:(c + 1) * LANES, :] = col[p * PAGE_SIZE:(p + 1) * PAGE_SIZE, :].T

    ka = _mm(xn, w_ref, slice(D_MODEL, 2 * D_MODEL))
    for c in range(N_PAIRS):
        kc = norm_rope(ka[:, c * LANES:(c + 1) * LANES], kg_ref[...])
        k_ref[:, c * LANES:(c + 1) * LANES] = kc
        if paged:
            put_paged(kt_ref, kc, c)
    va = _mm(xn, w_ref, slice(2 * D_MODEL, 3 * D_MODEL))
    v_ref[...] = va
    if paged:
        for c in range(N_PAIRS):
            put_paged(vt_ref, va[:, c * LANES:(c + 1) * LANES], c)


def _qkv_call(h2, layer, j, norm_g, w_qkv, qg, kg, cos_t, sin_t, *, t, planes, paged_kv=None):
    n = h2.shape[0]
    n_pos_tiles = cos_t.shape[0] // t
    paged = paged_kv is not None
    kern = functools.partial(_qkv_kernel, t=t, paged=paged)
    tok = pl.BlockSpec((t, D_MODEL), lambda i: (i, 0))
    tab = pl.BlockSpec((t, LANES), lambda i: (i % n_pos_tiles, 0))
    in_specs = [
        tok,
        pl.BlockSpec((None, 1, D_MODEL), lambda i: (layer, 0, 0)),
        pl.BlockSpec((None, planes, D_MODEL, 3 * D_MODEL), lambda i: (j, 0, 0, 0)),
        pl.BlockSpec((None, 1, LANES), lambda i: (j, 0, 0)),
        pl.BlockSpec((None, 1, LANES), lambda i: (j, 0, 0)),
        tab, tab,
    ]
    args = [h2, norm_g, w_qkv, qg, kg, cos_t, sin_t]
    out_specs = [tok, tok, tok]
    out_shape = [jax.ShapeDtypeStruct((n, D_MODEL), F32)] * 3
    aliases = {}
    if paged:
        assert t % PAGE_SIZE == 0
        ppt = t // PAGE_SIZE
        pg = pl.BlockSpec((None, ppt, D_MODEL, PAGE_SIZE), lambda i: (j, i, 0, 0))
        out_specs += [pg, pg]
        out_shape += [jax.ShapeDtypeStruct(a.shape, F32) for a in paged_kv]
        in_specs += [pl.BlockSpec(memory_space=pl.ANY)] * 2
        args += list(paged_kv)
        aliases = {7: 3, 8: 4}
    return pl.pallas_call(
        kern,
        grid=(n // t,),
        in_specs=in_specs,
        out_specs=out_specs,
        out_shape=out_shape,
        input_output_aliases=aliases,
        compiler_params=_params("arbitrary"),
        name="qkv_rope",
    )(*args)


def _top_blocks(gate, lanef):
    sel = jnp.zeros(gate.shape, F32)
    for _ in range(MOBA_TOPK):
        mx = jnp.max(gate, axis=-1, keepdims=True)
        idx = jnp.min(jnp.where(gate == mx, lanef, 1e9), axis=-1, keepdims=True)
        pick = (lanef == idx) & (mx > NEG_INF)
        sel = jnp.where(pick, 1.0, sel)
        gate = jnp.where(pick, NEG_INF, gate)
    return sel


def _attn_p_tile(c, q_ref, o_ref, kb_s, vb_s, km_s):
    blk = MOBA_BLOCK
    q = q_ref[0] * (HEAD_DIM ** -0.5)
    lane = lax.broadcasted_iota(jnp.int32, (blk, LANES), 1)
    lanef = lane.astype(F32)
    causal = (lax.broadcasted_iota(jnp.int32, (blk, blk), 1) <= lax.broadcasted_iota(jnp.int32, (blk, blk), 0))
    outs = []
    for j in range(2):
        qjf = jnp.where((lane < HEAD_DIM) if j == 0 else (lane >= HEAD_DIM), q, 0.0)
        qj = qjf.astype(BF16)
        blocks = [c] + list(range(c))
        scores = [jnp.where(causal, _dot_nt(qj, kb_s[c * blk:(c + 1) * blk, :]), NEG_INF)]
        if c > MOBA_TOPK:
            q_lo = (qjf - qj.astype(F32)).astype(BF16)
            gate = _dot_nt(qj, km_s[0]) + _dot_nt(q_lo, km_s[0]) + _dot_nt(qj, km_s[1])
            sel = _top_blocks(jnp.where(lane < c, gate, NEG_INF), lanef)
        for n in range(c):
            sn = _dot_nt(qj, kb_s[n * blk:(n + 1) * blk, :])
            if c > MOBA_TOPK:
                col = jnp.max(jnp.where(lane == n, sel, 0.0), axis=-1, keepdims=True)
                sn = jnp.where(col > 0.0, sn, NEG_INF)
            scores.append(sn)
        mx = scores[0]
        for sn in scores[1:]:
            mx = jnp.maximum(mx, sn)
        m = jnp.max(mx, axis=-1, keepdims=True)
        l_el = None
        acc = None
        for n, sn in zip(blocks, scores):
            p = jnp.exp(sn - m)
            pv = _dot(p.astype(BF16), vb_s[n * blk:(n + 1) * blk, :])
            l_el = p if l_el is None else l_el + p
            acc = pv if acc is None else acc + pv
        outs.append(acc / jnp.sum(l_el, axis=-1, keepdims=True))
    o_ref[0] = jnp.where(lane < HEAD_DIM, outs[0], outs[1]).astype(o_ref.dtype)


def _attn_p_kernel(q_ref, k_ref, v_ref, o_ref, kb_s, vb_s, km_s, km32_s, *, n_blk):
    i = pl.program_id(2)
    blk = MOBA_BLOCK

    @pl.when(i == 0)
    def _():
        kb_s[...] = k_ref[0].astype(BF16)
        vb_s[...] = v_ref[0].astype(BF16)
        km32_s[...] = jnp.zeros(km32_s.shape, F32)
        for n in range(n_blk):
            km32_s[n:n + 1, :] = jnp.mean(k_ref[0, n * blk:(n + 1) * blk, :], axis=0, keepdims=True)
        km = km32_s[...]
        km_hi = km.astype(BF16)
        km_s[0] = km_hi
        km_s[1] = (km - km_hi.astype(F32)).astype(BF16)

    for c in range(n_blk):
        @pl.when(i == c)
        def _(c=c):
            _attn_p_tile(c, q_ref, o_ref, kb_s, vb_s, km_s)


def _attn_p_call(q, k, v, out_dtype):
    bsz, s, _ = q.shape
    n_blk = s // MOBA_BLOCK
    assert n_blk <= LANES
    kern = functools.partial(_attn_p_kernel, n_blk=n_blk)
    qs = pl.BlockSpec((1, MOBA_BLOCK, LANES), lambda b, p, i: (b, i, p))
    kvs = pl.BlockSpec((1, s, LANES), lambda b, p, i: (b, 0, p))
    return pl.pallas_call(
        kern,
        grid=(bsz, N_PAIRS, n_blk),
        in_specs=[qs, kvs, kvs],
        out_specs=qs,
        out_shape=jax.ShapeDtypeStruct((bsz, s, D_MODEL), out_dtype),
        scratch_shapes=[
            pltpu.VMEM((s, LANES), BF16),
            pltpu.VMEM((s, LANES), BF16),
            pltpu.VMEM((2, LANES, LANES), BF16),
            pltpu.VMEM((LANES, LANES), F32),
        ],
        compiler_params=_params("arbitrary", "arbitrary", "arbitrary"),
        name="moba_prompt",
    )(q, k, v)


def _attn_s_kernel(pt_ref, q_ref, kn_ref, vn_ref, ck_hbm, cv_hbm, o_ref,
                   kbuf, vbuf, ksem, vsem, qbd_s, m_s, l_s, g_s, acc_s, sc_s, pr_s, *, layer, n_pages, t):
    b = pl.program_id(0)
    rows = 2 * t
    ppb = MOBA_BLOCK // PAGE_SIZE
    n_blocks = n_pages // ppb
    n_groups = n_pages // GROUP_PAGES
    n_slots = kbuf.shape[0]

    def copies(pg):
        slot = pg % n_slots
        src = pt_ref[b, pg]
        return (pltpu.make_async_copy(ck_hbm.at[layer, src], kbuf.at[slot], ksem.at[slot]),
                pltpu.make_async_copy(cv_hbm.at[layer, src], vbuf.at[slot], vsem.at[slot]))

    def start_group(g):
        for pg in range(g * GROUP_PAGES, (g + 1) * GROUP_PAGES):
            for cp in copies(pg):
                cp.start()

    def wait_group(g):
        for pg in range(g * GROUP_PAGES, (g + 1) * GROUP_PAGES):
            for cp in copies(pg):
                cp.wait()

    for g in range(min(GROUPS_IN_FLIGHT, n_groups)):
        start_group(g)

    lane = lax.broadcasted_iota(jnp.int32, (rows, LANES), 1)
    row = lax.broadcasted_iota(jnp.int32, (rows, LANES), 0)
    lanef = lane.astype(F32)
    head_mask = (lane < HEAD_DIM) == (row < t)
    q = q_ref[0] * (HEAD_DIM ** -0.5)
    qf = []
    for hp in range(N_PAIRS):
        qc = q[:, hp * LANES:(hp + 1) * LANES]
        qf.append(jnp.where(head_mask, jnp.concatenate([qc, qc], axis=0), 0.0))
        qbd_s[hp] = qf[hp].astype(BF16)
    m_s[...] = jnp.full(m_s.shape, NEG_INF, F32)
    l_s[...] = jnp.zeros(l_s.shape, F32)
    g_s[...] = jnp.zeros(g_s.shape, F32)

    bpg = GROUP_PAGES // ppb
    units = [(hp, bi) for hp in range(N_PAIRS) for bi in range(bpg)]
    for g in range(n_groups):
        wait_group(g)

        def unit_slots(bi):
            return [(g * GROUP_PAGES + bi * ppb + u) % n_slots for u in range(ppb)]

        for ui, (hp, bi) in enumerate(units):
            hs = slice(hp * LANES, (hp + 1) * LANES)
            for u, sl in enumerate(unit_slots(bi)):
                sc_s[ui, :, u * PAGE_SIZE:(u + 1) * PAGE_SIZE] = _dot(qbd_s[hp], kbuf[sl, hs, :].astype(BF16))
        for ui, (hp, bi) in enumerate(units):
            sc = sc_s[ui]
            m = jnp.max(sc, axis=-1, keepdims=True)
            p = jnp.exp(sc - m)
            pr_s[ui] = p.astype(BF16)
            here = lane == g * bpg + bi
            m_s[hp] = jnp.where(here, m, m_s[hp])
            l_s[hp] = jnp.where(here, jnp.sum(p, axis=-1, keepdims=True), l_s[hp])
            g_s[hp] = jnp.where(here, jnp.sum(sc, axis=-1, keepdims=True), g_s[hp])
        for ui, (hp, bi) in enumerate(units):
            hs = slice(hp * LANES, (hp + 1) * LANES)
            acc = None
            for u, sl in enumerate(unit_slots(bi)):
                pv = _dot_nt(pr_s[ui, :, u * PAGE_SIZE:(u + 1) * PAGE_SIZE], vbuf[sl, hs, :].astype(BF16))
                acc = pv if acc is None else acc + pv
            acc_s[g * bpg + bi, hp] = acc
        if g + GROUPS_IN_FLIGHT < n_groups:
            start_group(g + GROUPS_IN_FLIGHT)

    kn = kn_ref[0]
    vn = vn_ref[0]
    t_row = (row[:, 0:1] & (t - 1))
    cols = []
    for hp in range(N_PAIRS):
        kc = kn[:, hp * LANES:(hp + 1) * LANES]
        vc = vn[:, hp * LANES:(hp + 1) * LANES]
        s_own = []
        for t2 in range(t):
            st = jnp.sum(qf[hp] * kc[t2:t2 + 1, :], axis=-1, keepdims=True)
            s_own.append(jnp.where(t_row >= t2, st, NEG_INF))
        m_own = s_own[0]
        for t2 in range(1, t):
            m_own = jnp.maximum(m_own, s_own[t2])
        l_own = jnp.zeros((rows, 1), F32)
        acc_own = jnp.zeros((rows, LANES), F32)
        for t2 in range(t):
            pt2 = jnp.exp(s_own[t2] - m_own)
            l_own = l_own + pt2
            acc_own = acc_own + pt2 * vc[t2:t2 + 1, :]
        sel = _top_blocks(jnp.where(lane < n_blocks, g_s[hp], NEG_INF), lanef) > 0.0
        mm = m_s[hp]
        big = jnp.maximum(jnp.max(jnp.where(sel, mm, NEG_INF), axis=-1, keepdims=True), m_own)
        w = jnp.where(sel, jnp.exp(mm - big), 0.0)
        w_own = jnp.exp(m_own - big)
        den = jnp.sum(w * l_s[hp], axis=-1, keepdims=True) + w_own * l_own
        num = w_own * acc_own
        for n in range(n_blocks):
            num = num + w[:, n:n + 1] * acc_s[n, hp]
        o = num / den
        cols.append(jnp.where(lane[0:t] < HEAD_DIM, o[0:t], o[t:rows]))
    o_ref[0] = jnp.concatenate(cols, axis=1)


def _attn_s_call(page_table, q, k_new, v_new, ck, cv, layer):
    db, t, _ = q.shape
    n_pages = page_table.shape[1]
    n_blocks = n_pages * PAGE_SIZE // MOBA_BLOCK
    assert (n_pages * PAGE_SIZE) % MOBA_BLOCK == 0 and n_blocks <= LANES
    assert GROUP_PAGES % (MOBA_BLOCK // PAGE_SIZE) == 0 and n_pages % GROUP_PAGES == 0
    assert t & (t - 1) == 0
    kern = functools.partial(_attn_s_kernel, layer=layer, n_pages=n_pages, t=t)
    tok = pl.BlockSpec((1, t, D_MODEL), lambda b, pt: (b, 0, 0))
    rows = 2 * t
    n_slots = GROUPS_IN_FLIGHT * GROUP_PAGES
    n_units = N_PAIRS * GROUP_PAGES * PAGE_SIZE // MOBA_BLOCK
    grid_spec = pltpu.PrefetchScalarGridSpec(
        num_scalar_prefetch=1,
        grid=(db,),
        in_specs=[tok, tok, tok, pl.BlockSpec(memory_space=pl.ANY), pl.BlockSpec(memory_space=pl.ANY)],
        out_specs=tok,
        scratch_shapes=[
            pltpu.VMEM((n_slots, D_MODEL, PAGE_SIZE), F32),
            pltpu.VMEM((n_slots, D_MODEL, PAGE_SIZE), F32),
            pltpu.SemaphoreType.DMA((n_slots,)),
            pltpu.SemaphoreType.DMA((n_slots,)),
            pltpu.VMEM((N_PAIRS, rows, LANES), BF16),
            pltpu.VMEM((N_PAIRS, rows, LANES), F32),
            pltpu.VMEM((N_PAIRS, rows, LANES), F32),
            pltpu.VMEM((N_PAIRS, rows, LANES), F32),
            pltpu.VMEM((n_blocks, N_PAIRS, rows, LANES), F32),
            pltpu.VMEM((n_units, rows, MOBA_BLOCK), F32),
            pltpu.VMEM((n_units, rows, MOBA_BLOCK), BF16),
        ],
    )
    return pl.pallas_call(
        kern,
        grid_spec=grid_spec,
        out_shape=jax.ShapeDtypeStruct((db, t, D_MODEL), F32),
        compiler_params=_params("arbitrary"),
        name="moba_sample",
    )(page_table, q, k_new, v_new, ck, cv)


def _route(logits):
    lane = lax.broadcasted_iota(jnp.int32, logits.shape, 1)
    lanef = lane.astype(F32)
    is_g = (lane >= N_EXPERTS) & (lane < N_EXPERTS + N_GROUPS)
    lg = jnp.where(is_g, logits, NEG_INF)
    mg = jnp.max(lg, axis=-1, keepdims=True)
    grp = jnp.min(jnp.where(lg == mg, lanef, 1e9), axis=-1, keepdims=True) - float(N_EXPERTS)
    p_grp = 1.0 / jnp.sum(jnp.exp(lg - mg), axis=-1, keepdims=True)
    member = (lane < N_EXPERTS) & (lax.shift_right_logical(lane, 2).astype(F32) == grp)
    le = jnp.where(member, logits, NEG_INF)
    m1 = jnp.max(le, axis=-1, keepdims=True)
    i1 = jnp.min(jnp.where(le == m1, lanef, 1e9), axis=-1, keepdims=True)
    le2 = jnp.where(lanef == i1, NEG_INF, le)
    m2 = jnp.max(le2, axis=-1, keepdims=True)
    i2 = jnp.min(jnp.where(le2 == m2, lanef, 1e9), axis=-1, keepdims=True)
    e2 = jnp.exp(m2 - m1)
    den = 1.0 + e2
    return jnp.where(lanef == i1, p_grp / den, 0.0) + jnp.where(lanef == i2, p_grp * e2 / den, 0.0)


def _moe_kernel(*refs, has_proj, tm):
    if has_proj:
        h_ref, att_ref, wo_ref = refs[:3]
        refs = refs[3:]
    else:
        h_ref = refs[0]
        refs = refs[1:]
    g_ref, wr_ref, br_ref, wg_ref, wu_ref, wd_ref, o_ref, xn_s, comb_s, acc_s = refs
    gi = pl.program_id(1)

    @pl.when(gi == 0)
    def _():
        hm = h_ref[...]
        if has_proj:
            hm = hm + _mm(att_ref[...].astype(F32), wo_ref)
        acc_s[...] = hm
        xn = _rms(hm, g_ref[...])
        xn_s[...] = xn.astype(BF16)
        logits = jnp.dot(xn, wr_ref[...], precision=lax.Precision.HIGHEST, preferred_element_type=F32)
        comb_s[...] = _route(logits + br_ref[...])

    x = xn_s[...]
    lane = lax.broadcasted_iota(jnp.int32, (tm, LANES), 1)
    comb = comb_s[...]
    for e in range(EXPERTS_PER_GROUP):
        c = jnp.sum(jnp.where(lane == gi * EXPERTS_PER_GROUP + e, comb, 0.0), axis=-1, keepdims=True)
        hg = _dot(x, wg_ref[e])
        hh = hg * _sigmoid(hg) * _dot(x, wu_ref[e]) * c
        acc_s[...] += _dot(hh.astype(BF16), wd_ref[e])

    @pl.when(gi == N_GROUPS - 1)
    def _():
        o_ref[...] = acc_s[...]


def _moe_call(h2, layer, norm_g, wr, br, wg, wu, wd, att=None, w_o=None, j=0, planes=1, *, tm):
    n = h2.shape[0]
    has_proj = att is not None
    kern = functools.partial(_moe_kernel, has_proj=has_proj, tm=tm)
    tok = pl.BlockSpec((tm, D_MODEL), lambda i, g: (i, 0))
    in_specs = [tok]
    args = [h2]
    if has_proj:
        in_specs += [tok, pl.BlockSpec((None, planes, D_MODEL, D_MODEL), lambda i, g: (j, 0, 0, 0))]
        args += [att, w_o]
    in_specs += [
        pl.BlockSpec((None, 1, D_MODEL), lambda i, g: (layer, 0, 0)),
        pl.BlockSpec((None, D_MODEL, LANES), lambda i, g: (layer, 0, 0)),
        pl.BlockSpec((None, 1, LANES), lambda i, g: (layer, 0, 0)),
        pl.BlockSpec((None, None, EXPERTS_PER_GROUP, D_MODEL, D_EXPERT), lambda i, g: (layer, g, 0, 0, 0)),
        pl.BlockSpec((None, None, EXPERTS_PER_GROUP, D_MODEL, D_EXPERT), lambda i, g: (layer, g, 0, 0, 0)),
        pl.BlockSpec((None, None, EXPERTS_PER_GROUP, D_EXPERT, D_MODEL), lambda i, g: (layer, g, 0, 0, 0)),
    ]
    args += [norm_g, wr, br, wg, wu, wd]
    return pl.pallas_call(
        kern,
        grid=(n // tm, N_GROUPS),
        in_specs=in_specs,
        out_specs=tok,
        out_shape=jax.ShapeDtypeStruct((n, D_MODEL), F32),
        scratch_shapes=[
            pltpu.VMEM((tm, D_MODEL), BF16),
            pltpu.VMEM((tm, LANES), F32),
            pltpu.VMEM((tm, D_MODEL), F32),
        ],
        compiler_params=_params("arbitrary", "arbitrary"),
        name="moe_proj" if has_proj else "moe",
    )(*args)


def _rope_tables(pos):
    half = HEAD_DIM // 2
    inv = ROPE_THETA ** (-(jnp.arange(half, dtype=F32) * 2.0 / HEAD_DIM))
    ang = pos.astype(F32)[:, None] * inv[None, :]
    cos = jnp.cos(ang)
    sin = jnp.sin(ang)
    cos_h = jnp.concatenate([cos, cos], axis=-1)
    sin_h = jnp.concatenate([-sin, sin], axis=-1)
    return jnp.concatenate([cos_h, cos_h], axis=-1), jnp.concatenate([sin_h, sin_h], axis=-1)


def kernel(x_prompt, x_sample, cache_k, cache_v, page_table, state_conv_a, state_conv_b, norm_mix_g, norm_ffn_g,
           w_in_ab, conv_a_w, conv_a_b, ln_a_g, ln_a_b, conv_b_w, w_out_ab, w_qkv, q_norm_g, k_norm_g, w_o,
           router_g_w, router_g_b, router_e_w, router_e_b, w_gate, w_up, w_down):
    bsz, s, _ = x_prompt.shape
    db, t, _ = x_sample.shape
    depth = norm_mix_g.shape[0]
    n_pages = page_table.shape[1]
    past = n_pages * PAGE_SIZE
    n_pool = cache_k.shape[1]

    w_in_b = _planes(w_in_ab, True)
    w_out_b = _planes(w_out_ab, True)
    w_qkv_b = _planes(w_qkv, True)
    w_o_b = _planes(w_o, True)
    wg_b = w_gate.astype(BF16).reshape(depth, N_GROUPS, EXPERTS_PER_GROUP, D_MODEL, D_EXPERT)
    wu_b = w_up.astype(BF16).reshape(depth, N_GROUPS, EXPERTS_PER_GROUP, D_MODEL, D_EXPERT)
    wd_b = w_down.astype(BF16).reshape(depth, N_GROUPS, EXPERTS_PER_GROUP, D_EXPERT, D_MODEL)
    pad = LANES - N_EXPERTS - N_GROUPS
    wr = jnp.concatenate([router_e_w, router_g_w, jnp.zeros((depth, D_MODEL, pad), F32)], axis=-1)
    br = jnp.concatenate([router_e_b, router_g_b, jnp.zeros((depth, pad), F32)], axis=-1)[:, None, :]
    qg = jnp.concatenate([q_norm_g, q_norm_g], axis=-1)[:, None, :]
    kg = jnp.concatenate([k_norm_g, k_norm_g], axis=-1)[:, None, :]
    norm_mix_g = norm_mix_g[:, None, :]
    norm_ffn_g = norm_ffn_g[:, None, :]
    conv_a_b = conv_a_b[:, None, :]
    ln_a_g = ln_a_g[:, None, :]
    ln_a_b = ln_a_b[:, None, :]
    cos_p, sin_p = _rope_tables(jnp.arange(s, dtype=jnp.int32))
    cos_s, sin_s = _rope_tables(past + jnp.arange(t, dtype=jnp.int32))
    cos_s = jnp.tile(cos_s, (db, 1))
    sin_s = jnp.tile(sin_s, (db, 1))
    ck = jnp.transpose(cache_k, (0, 1, 3, 4, 2)).reshape(cache_k.shape[0], n_pool, D_MODEL, PAGE_SIZE)
    cv = jnp.transpose(cache_v, (0, 1, 3, 4, 2)).reshape(cache_v.shape[0], n_pool, D_MODEL, PAGE_SIZE)

    hp, hs = x_prompt, x_sample
    n_attn = depth // 2
    kt_all = jnp.zeros((n_attn, bsz * s // PAGE_SIZE, D_MODEL, PAGE_SIZE), F32)
    vt_all = jnp.zeros((n_attn, bsz * s // PAGE_SIZE, D_MODEL, PAGE_SIZE), F32)
    ks_l, vs_l = [], []
    cap_l, cbp_l, cas_l, cbs_l = [], [], [], []
    moe_w = (norm_ffn_g, wr, br, wg_b, wu_b, wd_b)
    for layer in range(depth):
        j = layer // 2
        pp = 2 if layer < HI_PRECISION_LAYERS else 1
        if layer % 2 == 0:
            ab_w = (norm_mix_g, w_in_b, conv_a_w, conv_a_b, ln_a_g, ln_a_b, conv_b_w, w_out_b)
            zero_a = jnp.zeros((bsz, HALO_A, D_CONV), F32)
            zero_b = jnp.zeros((bsz, HALO_B, D_CONV), F32)
            hp, a_p, b_p = _ab_call(hp, zero_a, zero_b, layer, j, *ab_w, bb=1, t=256, planes=pp)
            prev_a = jnp.pad(state_conv_a[j], ((0, 0), (HALO_A - (CONV_A_WIDTH - 1), 0), (0, 0)))
            prev_b = jnp.pad(state_conv_b[j], ((0, 0), (HALO_B - (CONV_B_WIDTH - 1), 0), (0, 0)))
            hs, a_s, b_s = _ab_call(hs, prev_a, prev_b, layer, j, *ab_w, bb=db, t=t, planes=pp)
            cap_l.append(a_p[:, HALO_A - (CONV_A_WIDTH - 1):])
            cbp_l.append(b_p[:, HALO_B - (CONV_B_WIDTH - 1):])
            cas_l.append(a_s[:, HALO_A - (CONV_A_WIDTH - 1):])
            cbs_l.append(b_s[:, HALO_B - (CONV_B_WIDTH - 1):])
            hp = _moe_call(hp.reshape(bsz * s, D_MODEL), layer, *moe_w, tm=512).reshape(bsz, s, D_MODEL)
            hs = _moe_call(hs.reshape(db * t, D_MODEL), layer, *moe_w, tm=db * t).reshape(db, t, D_MODEL)
        else:
            q, k, v, kt_all, vt_all = _qkv_call(hp.reshape(bsz * s, D_MODEL), layer, j, norm_mix_g, w_qkv_b, qg, kg,
                                                cos_p, sin_p, t=256, planes=pp, paged_kv=(kt_all, vt_all))
            att = _attn_p_call(q.reshape(bsz, s, D_MODEL), k.reshape(bsz, s, D_MODEL), v.reshape(bsz, s, D_MODEL),
                               F32 if pp == 2 else BF16)
            hp = _moe_call(hp.reshape(bsz * s, D_MODEL), layer, *moe_w, att=att.reshape(bsz * s, D_MODEL),
                           w_o=w_o_b, j=j, planes=pp, tm=512).reshape(bsz, s, D_MODEL)
            q, k, v = _qkv_call(hs.reshape(db * t, D_MODEL), layer, j, norm_mix_g, w_qkv_b, qg, kg,
                                cos_s, sin_s, t=db * t, planes=pp)
            att = _attn_s_call(page_table, q.reshape(db, t, D_MODEL), k.reshape(db, t, D_MODEL),
                               v.reshape(db, t, D_MODEL), ck, cv, j)
            ks_l.append(k.reshape(db, t, N_HEADS, HEAD_DIM))
            vs_l.append(v.reshape(db, t, N_HEADS, HEAD_DIM))
            hs = _moe_call(hs.reshape(db * t, D_MODEL), layer, *moe_w, att=att.reshape(db * t, D_MODEL),
                           w_o=w_o_b, j=j, planes=pp, tm=db * t).reshape(db, t, D_MODEL)

    def unpage(a):
        return jnp.transpose(a.reshape(a.shape[0], a.shape[1], N_HEADS, HEAD_DIM, PAGE_SIZE), (0, 1, 4, 2, 3))

    return (hp, hs, unpage(kt_all), unpage(vt_all), jnp.stack(ks_l), jnp.stack(vs_l),
            jnp.stack(cap_l), jnp.stack(cbp_l), jnp.stack(cas_l), jnp.stack(cbs_l))
```

```python
import functools

import jax
import jax.numpy as jnp
from jax import lax
from jax.experimental import pallas as pl
from jax.experimental.pallas import tpu as pltpu

F32 = jnp.float32
BF16 = jnp.bfloat16

D_MODEL = 1024
N_HEADS = 16
HEAD_DIM = 64
N_PAIRS = N_HEADS // 2
ROPE_THETA = 10000.0
MOBA_BLOCK = 256
MOBA_TOPK = 3
PAGE_SIZE = 128
D_CONV = 512
CONV_A_WIDTH = 31
CONV_B_WIDTH = 3
HALO_A = 32
HALO_B = 8
N_GROUPS = 4
EXPERTS_PER_GROUP = 4
N_EXPERTS = 16
D_EXPERT = 256
NORM_EPS = 1e-6
NEG_INF = float("-inf")
LANES = 128
SUBLANES = 8
VMEM_LIMIT = 56 * 1024 * 1024
GROUP_PAGES = 4
GROUPS_IN_FLIGHT = 3
HI_PRECISION_LAYERS = 2


def _dot(a, b):
    return jnp.dot(a, b, preferred_element_type=F32)


def _dot_nt(a, b):
    return lax.dot_general(a, b, (((1,), (1,)), ((), ())), preferred_element_type=F32)


def _mm(x, w_ref, cols=slice(None)):
    x_hi = x.astype(BF16)
    y = _dot(x_hi, w_ref[0, :, cols])
    if w_ref.shape[0] == 2:
        x_lo = (x - x_hi.astype(F32)).astype(BF16)
        y = y + _dot(x_lo, w_ref[0, :, cols]) + _dot(x_hi, w_ref[1, :, cols])
    return y


def _planes(w, hi):
    w_hi = w.astype(BF16)
    if not hi:
        return w_hi[..., None, :, :]
    w_lo = (w - w_hi.astype(F32)).astype(BF16)
    return jnp.stack([w_hi, w_lo], axis=-3)


def _rms(x, g):
    return x * lax.rsqrt(jnp.mean(x * x, axis=-1, keepdims=True) + NORM_EPS) * g


def _sigmoid(x):
    return 1.0 / (1.0 + jnp.exp(-x))


def _params(*sem):
    return pltpu.CompilerParams(dimension_semantics=sem, vmem_limit_bytes=VMEM_LIMIT)


def _ab_kernel(h_ref, pa_ref, pb_ref, g_ref, win_ref, caw_ref, cab_ref, lng_ref, lnb_ref, cbw_ref, wout_ref,
               o_ref, na_ref, nb_ref, sa_ref, sb_ref, ab_ref, *maybe_sh, bb, t, r, n_tiles):
    i = pl.program_id(1)
    n = bb * t

    @pl.when(i == 0)
    def _():
        sa_ref[:, 0:HALO_A, :] = pa_ref[...]
        sb_ref[:, 0:HALO_B, :] = pb_ref[...]

    x = h_ref[...].reshape(n, D_MODEL)
    xn = _rms(x, g_ref[...])
    a = _mm(xn, win_ref, slice(0, 512)) * _sigmoid(_mm(xn, win_ref, slice(512, 1024)))
    sa_ref[:, HALO_A:HALO_A + t, :] = a.reshape(bb, t, D_CONV)
    bg = _mm(xn, win_ref, slice(1024, 1536))
    ch = _mm(xn, win_ref, slice(1536, 2048)) * _mm(xn, win_ref, slice(2048, 2560))
    sb_ref[:, HALO_B:HALO_B + t, :] = ch.reshape(bb, t, D_CONV)

    off_b = HALO_B - (CONV_B_WIDTH - 1)
    cb = cbw_ref[0:1, :] * sb_ref[:, off_b:off_b + t, :]
    for k in range(1, CONV_B_WIDTH):
        cb = cb + cbw_ref[k:k + 1, :] * sb_ref[:, off_b + k:off_b + k + t, :]
    ab_ref[:, D_CONV:2 * D_CONV] = bg * cb.reshape(n, D_CONV)

    off_a = HALO_A - (CONV_A_WIDTH - 1)
    sh_ref = maybe_sh[0] if maybe_sh else None
    if sh_ref is not None:
        for s in range(1, SUBLANES):
            sh_ref[s - 1] = sa_ref[0, s:s + sh_ref.shape[1], :]
    for b in range(bb):
        for c in range(t // r):
            acc = None
            for k in range(CONV_A_WIDTH):
                kk = k + off_a
                if sh_ref is not None and kk % SUBLANES:
                    base = c * r + kk // SUBLANES * SUBLANES
                    tap = sh_ref[kk % SUBLANES - 1, base:base + r, :]
                else:
                    tap = sa_ref[b, c * r + kk:c * r + kk + r, :]
                term = caw_ref[k:k + 1, :] * tap
                acc = term if acc is None else acc + term
            acc = acc + cab_ref[...]
            mu = jnp.mean(acc, axis=-1, keepdims=True)
            xc = acc - mu
            y = xc * lax.rsqrt(jnp.mean(xc * xc, axis=-1, keepdims=True) + NORM_EPS) * lng_ref[...] + lnb_ref[...]
            ab_ref[b * t + c * r:b * t + c * r + r, 0:D_CONV] = y * _sigmoid(y)

    y = _mm(ab_ref[...], wout_ref)
    o_ref[...] = (x + y).reshape(bb, t, D_MODEL)

    na_ref[...] = sa_ref[:, t:t + HALO_A, :]
    nb_ref[...] = sb_ref[:, t:t + HALO_B, :]
    if n_tiles > 1:
        sa_ref[:, 0:HALO_A, :] = sa_ref[:, t:t + HALO_A, :]
        sb_ref[:, 0:HALO_B, :] = sb_ref[:, t:t + HALO_B, :]


def _ab_call(h, prev_a, prev_b, layer, j, norm_g, w_in, ca_w, ca_b, ln_g, ln_b, cb_w, w_out, *, bb, t, planes):
    bsz, s, _ = h.shape
    n_tiles = s // t
    r = min(32, t)
    d_in = w_in.shape[-1]
    kern = functools.partial(_ab_kernel, bb=bb, t=t, r=r, n_tiles=n_tiles)
    scratch = [
        pltpu.VMEM((bb, t + HALO_A, D_CONV), F32),
        pltpu.VMEM((bb, t + HALO_B, D_CONV), F32),
        pltpu.VMEM((bb * t, 2 * D_CONV), F32),
    ]
    if bb == 1 and t >= HALO_A:
        scratch.append(pltpu.VMEM((SUBLANES - 1, t + HALO_A - SUBLANES, D_CONV), F32))
    return pl.pallas_call(
        kern,
        grid=(bsz // bb, n_tiles),
        in_specs=[
            pl.BlockSpec((bb, t, D_MODEL), lambda b, i: (b, i, 0)),
            pl.BlockSpec((bb, HALO_A, D_CONV), lambda b, i: (b, 0, 0)),
            pl.BlockSpec((bb, HALO_B, D_CONV), lambda b, i: (b, 0, 0)),
            pl.BlockSpec((None, 1, D_MODEL), lambda b, i: (layer, 0, 0)),
            pl.BlockSpec((None, planes, D_MODEL, d_in), lambda b, i: (j, 0, 0, 0)),
            pl.BlockSpec((None, CONV_A_WIDTH, D_CONV), lambda b, i: (j, 0, 0)),
            pl.BlockSpec((None, 1, D_CONV), lambda b, i: (j, 0, 0)),
            pl.BlockSpec((None, 1, D_CONV), lambda b, i: (j, 0, 0)),
            pl.BlockSpec((None, 1, D_CONV), lambda b, i: (j, 0, 0)),
            pl.BlockSpec((None, CONV_B_WIDTH, D_CONV), lambda b, i: (j, 0, 0)),
            pl.BlockSpec((None, planes, 2 * D_CONV, D_MODEL), lambda b, i: (j, 0, 0, 0)),
        ],
        out_specs=[
            pl.BlockSpec((bb, t, D_MODEL), lambda b, i: (b, i, 0)),
            pl.BlockSpec((bb, HALO_A, D_CONV), lambda b, i: (b, 0, 0)),
            pl.BlockSpec((bb, HALO_B, D_CONV), lambda b, i: (b, 0, 0)),
        ],
        out_shape=[
            jax.ShapeDtypeStruct(h.shape, F32),
            jax.ShapeDtypeStruct((bsz, HALO_A, D_CONV), F32),
            jax.ShapeDtypeStruct((bsz, HALO_B, D_CONV), F32),
        ],
        scratch_shapes=scratch,
        compiler_params=_params("arbitrary", "arbitrary"),
        name="ab_mixers",
    )(h, prev_a, prev_b, norm_g, w_in, ca_w, ca_b, ln_g, ln_b, cb_w, w_out)


def _qkv_kernel(*refs, t, paged):
    h_ref, g_ref, w_ref, qg_ref, kg_ref, cos_ref, sin_ref = refs[:7]
    outs = refs[9:] if paged else refs[7:]
    q_ref, k_ref, v_ref = outs[:3]
    kt_ref, vt_ref = outs[3:] if paged else (None, None)
    xn = _rms(h_ref[...], g_ref[...])
    lane = lax.broadcasted_iota(jnp.int32, (t, LANES), 1)
    lo = lane < HEAD_DIM
    first = (lane & (HEAD_DIM - 1)) < HEAD_DIM // 2
    cos = cos_ref[...]
    sin = sin_ref[...]

    def norm_rope(col, gg):
        sq = col * col
        s0 = jnp.sum(jnp.where(lo, sq, 0.0), axis=-1, keepdims=True)
        s1 = jnp.sum(jnp.where(lo, 0.0, sq), axis=-1, keepdims=True)
        ms = jnp.where(lo, s0, s1) * (1.0 / HEAD_DIM)
        y = col * lax.rsqrt(ms + NORM_EPS) * gg
        rot = jnp.where(first, pltpu.roll(y, LANES - HEAD_DIM // 2, 1), pltpu.roll(y, HEAD_DIM // 2, 1))
        return y * cos + rot * sin

    qa = _mm(xn, w_ref, slice(0, D_MODEL))
    for c in range(N_PAIRS):
        q_ref[:, c * LANES:(c + 1) * LANES] = norm_rope(qa[:, c * LANES:(c + 1) * LANES], qg_ref[...])
    def put_paged(dst_ref, col, c):
        for p in range(t // PAGE_SIZE):
            dst_ref[p, c * LANES:(c + 1) * LANES, :] = col[p * PAGE_SIZE:(p + 1) * PAGE_SIZE, :].T

    ka = _mm(xn, w_ref, slice(D_MODEL, 2 * D_MODEL))
    for c in range(N_PAIRS):
        kc = norm_rope(ka[:, c * LANES:(c + 1) * LANES], kg_ref[...])
        k_ref[:, c * LANES:(c + 1) * LANES] = kc
        if paged:
            put_paged(kt_ref, kc, c)
    va = _mm(xn, w_ref, slice(2 * D_MODEL, 3 * D_MODEL))
    v_ref[...] = va
    if paged:
        for c in range(N_PAIRS):
            put_paged(vt_ref, va[:, c * LANES:(c + 1) * LANES], c)


def _qkv_call(h2, layer, j, norm_g, w_qkv, qg, kg, cos_t, sin_t, *, t, planes, paged_kv=None):
    n = h2.shape[0]
    n_pos_tiles = cos_t.shape[0] // t
    paged = paged_kv is not None
    kern = functools.partial(_qkv_kernel, t=t, paged=paged)
    tok = pl.BlockSpec((t, D_MODEL), lambda i: (i, 0))
    tab = pl.BlockSpec((t, LANES), lambda i: (i % n_pos_tiles, 0))
    in_specs = [
        tok,
        pl.BlockSpec((None, 1, D_MODEL), lambda i: (layer, 0, 0)),
        pl.BlockSpec((None, planes, D_MODEL, 3 * D_MODEL), lambda i: (j, 0, 0, 0)),
        pl.BlockSpec((None, 1, LANES), lambda i: (j, 0, 0)),
        pl.BlockSpec((None, 1, LANES), lambda i: (j, 0, 0)),
        tab, tab,
    ]
    args = [h2, norm_g, w_qkv, qg, kg, cos_t, sin_t]
    out_specs = [tok, tok, tok]
    out_shape = [jax.ShapeDtypeStruct((n, D_MODEL), F32)] * 3
    aliases = {}
    if paged:
        assert t % PAGE_SIZE == 0
        ppt = t // PAGE_SIZE
        pg = pl.BlockSpec((None, ppt, D_MODEL, PAGE_SIZE), lambda i: (j, i, 0, 0))
        out_specs += [pg, pg]
        out_shape += [jax.ShapeDtypeStruct(a.shape, F32) for a in paged_kv]
        in_specs += [pl.BlockSpec(memory_space=pl.ANY)] * 2
        args += list(paged_kv)
        aliases = {7: 3, 8: 4}
    return pl.pallas_call(
        kern,
        grid=(n // t,),
        in_specs=in_specs,
        out_specs=out_specs,
        out_shape=out_shape,
        input_output_aliases=aliases,
        compiler_params=_params("arbitrary"),
        name="qkv_rope",
    )(*args)


def _top_blocks(gate, lanef):
    sel = jnp.zeros(gate.shape, F32)
    for _ in range(MOBA_TOPK):
        mx = jnp.max(gate, axis=-1, keepdims=True)
        idx = jnp.min(jnp.where(gate == mx, lanef, 1e9), axis=-1, keepdims=True)
        pick = (lanef == idx) & (mx > NEG_INF)
        sel = jnp.where(pick, 1.0, sel)
        gate = jnp.where(pick, NEG_INF, gate)
    return sel


def _attn_p_tile(c, q_ref, o_ref, kb_s, vb_s, km_s):
    blk = MOBA_BLOCK
    q = q_ref[0] * (HEAD_DIM ** -0.5)
    lane = lax.broadcasted_iota(jnp.int32, (blk, LANES), 1)
    lanef = lane.astype(F32)
    causal = (lax.broadcasted_iota(jnp.int32, (blk, blk), 1) <= lax.broadcasted_iota(jnp.int32, (blk, blk), 0))
    outs = []
    for j in range(2):
        qjf = jnp.where((lane < HEAD_DIM) if j == 0 else (lane >= HEAD_DIM), q, 0.0)
        qj = qjf.astype(BF16)
        blocks = [c] + list(range(c))
        scores = [jnp.where(causal, _dot_nt(qj, kb_s[c * blk:(c + 1) * blk, :]), NEG_INF)]
        if c > MOBA_TOPK:
            q_lo = (qjf - qj.astype(F32)).astype(BF16)
            gate = _dot_nt(qj, km_s[0]) + _dot_nt(q_lo, km_s[0]) + _dot_nt(qj, km_s[1])
            sel = _top_blocks(jnp.where(lane < c, gate, NEG_INF), lanef)
        for n in range(c):
            sn = _dot_nt(qj, kb_s[n * blk:(n + 1) * blk, :])
            if c > MOBA_TOPK:
                col = jnp.max(jnp.where(lane == n, sel, 0.0), axis=-1, keepdims=True)
                sn = jnp.where(col > 0.0, sn, NEG_INF)
            scores.append(sn)
        mx = scores[0]
        for sn in scores[1:]:
            mx = jnp.maximum(mx, sn)
        m = jnp.max(mx, axis=-1, keepdims=True)
        l_el = None
        acc = None
        for n, sn in zip(blocks, scores):
            p = jnp.exp(sn - m)
            pv = _dot(p.astype(BF16), vb_s[n * blk:(n + 1) * blk, :])
            l_el = p if l_el is None else l_el + p
            acc = pv if acc is None else acc + pv
        outs.append(acc / jnp.sum(l_el, axis=-1, keepdims=True))
    o_ref[0] = jnp.where(lane < HEAD_DIM, outs[0], outs[1]).astype(o_ref.dtype)


def _attn_p_kernel(q_ref, k_ref, v_ref, o_ref, kb_s, vb_s, km_s, km32_s, *, n_blk):
    blk = MOBA_BLOCK
    kb_s[...] = k_ref[0].astype(BF16)
    vb_s[...] = v_ref[0].astype(BF16)
    km32_s[...] = jnp.zeros(km32_s.shape, F32)
    for n in range(n_blk):
        km32_s[n:n + 1, :] = jnp.mean(k_ref[0, n * blk:(n + 1) * blk, :], axis=0, keepdims=True)
    km = km32_s[...]
    km_hi = km.astype(BF16)
    km_s[0] = km_hi
    km_s[1] = (km - km_hi.astype(F32)).astype(BF16)

    for c in range(n_blk):
        _attn_p_tile(c, q_ref.at[:, c * blk:(c + 1) * blk, :], o_ref.at[:, c * blk:(c + 1) * blk, :],
                     kb_s, vb_s, km_s)


def _attn_p_call(q, k, v, out_dtype):
    bsz, s, _ = q.shape
    n_blk = s // MOBA_BLOCK
    assert n_blk <= LANES
    kern = functools.partial(_attn_p_kernel, n_blk=n_blk)
    kvs = pl.BlockSpec((1, s, LANES), lambda b, p: (b, 0, p))
    qs = kvs
    return pl.pallas_call(
        kern,
        grid=(bsz, N_PAIRS),
        in_specs=[qs, kvs, kvs],
        out_specs=qs,
        out_shape=jax.ShapeDtypeStruct((bsz, s, D_MODEL), out_dtype),
        scratch_shapes=[
            pltpu.VMEM((s, LANES), BF16),
            pltpu.VMEM((s, LANES), BF16),
            pltpu.VMEM((2, LANES, LANES), BF16),
            pltpu.VMEM((LANES, LANES), F32),
        ],
        compiler_params=_params("arbitrary", "arbitrary"),
        name="moba_prompt",
    )(q, k, v)


def _attn_s_kernel(pt_ref, q_ref, kn_ref, vn_ref, ck_hbm, cv_hbm, o_ref,
                   kbuf, vbuf, ksem, vsem, qbd_s, m_s, l_s, g_s, acc_s, sc_s, pr_s, *, layer, n_pages, t):
    b = pl.program_id(0)
    rows = 2 * t
    ppb = MOBA_BLOCK // PAGE_SIZE
    n_blocks = n_pages // ppb
    n_groups = n_pages // GROUP_PAGES
    n_slots = kbuf.shape[0]

    def copies(pg):
        slot = pg % n_slots
        src = pt_ref[b, pg]
        return (pltpu.make_async_copy(ck_hbm.at[layer, src], kbuf.at[slot], ksem.at[slot]),
                pltpu.make_async_copy(cv_hbm.at[layer, src], vbuf.at[slot], vsem.at[slot]))

    def start_group(g):
        for pg in range(g * GROUP_PAGES, (g + 1) * GROUP_PAGES):
            for cp in copies(pg):
                cp.start()

    def wait_group(g):
        for pg in range(g * GROUP_PAGES, (g + 1) * GROUP_PAGES):
            for cp in copies(pg):
                cp.wait()

    for g in range(min(GROUPS_IN_FLIGHT, n_groups)):
        start_group(g)

    lane = lax.broadcasted_iota(jnp.int32, (rows, LANES), 1)
    row = lax.broadcasted_iota(jnp.int32, (rows, LANES), 0)
    lanef = lane.astype(F32)
    head_mask = (lane < HEAD_DIM) == (row < t)
    q = q_ref[0] * (HEAD_DIM ** -0.5)
    qf = []
    for hp in range(N_PAIRS):
        qc = q[:, hp * LANES:(hp + 1) * LANES]
        qf.append(jnp.where(head_mask, jnp.concatenate([qc, qc], axis=0), 0.0))
        qbd_s[hp] = qf[hp].astype(BF16)
    m_s[...] = jnp.full(m_s.shape, NEG_INF, F32)
    l_s[...] = jnp.zeros(l_s.shape, F32)
    g_s[...] = jnp.zeros(g_s.shape, F32)

    bpg = GROUP_PAGES // ppb
    units = [(hp, bi) for hp in range(N_PAIRS) for bi in range(bpg)]
    for g in range(n_groups):
        wait_group(g)

        def unit_slots(bi):
            return [(g * GROUP_PAGES + bi * ppb + u) % n_slots for u in range(ppb)]

        for ui, (hp, bi) in enumerate(units):
            hs = slice(hp * LANES, (hp + 1) * LANES)
            for u, sl in enumerate(unit_slots(bi)):
                sc_s[ui, :, u * PAGE_SIZE:(u + 1) * PAGE_SIZE] = _dot(qbd_s[hp], kbuf[sl, hs, :].astype(BF16))
        for ui, (hp, bi) in enumerate(units):
            sc = sc_s[ui]
            m = jnp.max(sc, axis=-1, keepdims=True)
            p = jnp.exp(sc - m)
            pr_s[ui] = p.astype(BF16)
            here = lane == g * bpg + bi
            m_s[hp] = jnp.where(here, m, m_s[hp])
            l_s[hp] = jnp.where(here, jnp.sum(p, axis=-1, keepdims=True), l_s[hp])
            g_s[hp] = jnp.where(here, jnp.sum(sc, axis=-1, keepdims=True), g_s[hp])
        for ui, (hp, bi) in enumerate(units):
            hs = slice(hp * LANES, (hp + 1) * LANES)
            acc = None
            for u, sl in enumerate(unit_slots(bi)):
                pv = _dot_nt(pr_s[ui, :, u * PAGE_SIZE:(u + 1) * PAGE_SIZE], vbuf[sl, hs, :].astype(BF16))
                acc = pv if acc is None else acc + pv
            acc_s[g * bpg + bi, hp] = acc
        if g + GROUPS_IN_FLIGHT < n_groups:
            start_group(g + GROUPS_IN_FLIGHT)

    kn = kn_ref[0]
    vn = vn_ref[0]
    t_row = (row[:, 0:1] & (t - 1))
    cols = []
    for hp in range(N_PAIRS):
        kc = kn[:, hp * LANES:(hp + 1) * LANES]
        vc = vn[:, hp * LANES:(hp + 1) * LANES]
        s_own = []
        for t2 in range(t):
            st = jnp.sum(qf[hp] * kc[t2:t2 + 1, :], axis=-1, keepdims=True)
            s_own.append(jnp.where(t_row >= t2, st, NEG_INF))
        m_own = s_own[0]
        for t2 in range(1, t):
            m_own = jnp.maximum(m_own, s_own[t2])
        l_own = jnp.zeros((rows, 1), F32)
        acc_own = jnp.zeros((rows, LANES), F32)
        for t2 in range(t):
            pt2 = jnp.exp(s_own[t2] - m_own)
            l_own = l_own + pt2
            acc_own = acc_own + pt2 * vc[t2:t2 + 1, :]
        sel = _top_blocks(jnp.where(lane < n_blocks, g_s[hp], NEG_INF), lanef) > 0.0
        mm = m_s[hp]
        big = jnp.maximum(jnp.max(jnp.where(sel, mm, NEG_INF), axis=-1, keepdims=True), m_own)
        w = jnp.where(sel, jnp.exp(mm - big), 0.0)
        w_own = jnp.exp(m_own - big)
        den = jnp.sum(w * l_s[hp], axis=-1, keepdims=True) + w_own * l_own
        num = w_own * acc_own
        for n in range(n_blocks):
            num = num + w[:, n:n + 1] * acc_s[n, hp]
        o = num / den
        cols.append(jnp.where(lane[0:t] < HEAD_DIM, o[0:t], o[t:rows]))
    o_ref[0] = jnp.concatenate(cols, axis=1)


def _attn_s_call(page_table, q, k_new, v_new, ck, cv, layer):
    db, t, _ = q.shape
    n_pages = page_table.shape[1]
    n_blocks = n_pages * PAGE_SIZE // MOBA_BLOCK
    assert (n_pages * PAGE_SIZE) % MOBA_BLOCK == 0 and n_blocks <= LANES
    assert GROUP_PAGES % (MOBA_BLOCK // PAGE_SIZE) == 0 and n_pages % GROUP_PAGES == 0
    assert t & (t - 1) == 0
    kern = functools.partial(_attn_s_kernel, layer=layer, n_pages=n_pages, t=t)
    tok = pl.BlockSpec((1, t, D_MODEL), lambda b, pt: (b, 0, 0))
    rows = 2 * t
    n_slots = GROUPS_IN_FLIGHT * GROUP_PAGES
    n_units = N_PAIRS * GROUP_PAGES * PAGE_SIZE // MOBA_BLOCK
    grid_spec = pltpu.PrefetchScalarGridSpec(
        num_scalar_prefetch=1,
        grid=(db,),
        in_specs=[tok, tok, tok, pl.BlockSpec(memory_space=pl.ANY), pl.BlockSpec(memory_space=pl.ANY)],
        out_specs=tok,
        scratch_shapes=[
            pltpu.VMEM((n_slots, D_MODEL, PAGE_SIZE), F32),
            pltpu.VMEM((n_slots, D_MODEL, PAGE_SIZE), F32),
            pltpu.SemaphoreType.DMA((n_slots,)),
            pltpu.SemaphoreType.DMA((n_slots,)),
            pltpu.VMEM((N_PAIRS, rows, LANES), BF16),
            pltpu.VMEM((N_PAIRS, rows, LANES), F32),
            pltpu.VMEM((N_PAIRS, rows, LANES), F32),
            pltpu.VMEM((N_PAIRS, rows, LANES), F32),
            pltpu.VMEM((n_blocks, N_PAIRS, rows, LANES), F32),
            pltpu.VMEM((n_units, rows, MOBA_BLOCK), F32),
            pltpu.VMEM((n_units, rows, MOBA_BLOCK), BF16),
        ],
    )
    return pl.pallas_call(
        kern,
        grid_spec=grid_spec,
        out_shape=jax.ShapeDtypeStruct((db, t, D_MODEL), F32),
        compiler_params=_params("arbitrary"),
        name="moba_sample",
    )(page_table, q, k_new, v_new, ck, cv)


def _route(logits):
    lane = lax.broadcasted_iota(jnp.int32, logits.shape, 1)
    lanef = lane.astype(F32)
    is_g = (lane >= N_EXPERTS) & (lane < N_EXPERTS + N_GROUPS)
    lg = jnp.where(is_g, logits, NEG_INF)
    mg = jnp.max(lg, axis=-1, keepdims=True)
    grp = jnp.min(jnp.where(lg == mg, lanef, 1e9), axis=-1, keepdims=True) - float(N_EXPERTS)
    p_grp = 1.0 / jnp.sum(jnp.exp(lg - mg), axis=-1, keepdims=True)
    member = (lane < N_EXPERTS) & (lax.shift_right_logical(lane, 2).astype(F32) == grp)
    le = jnp.where(member, logits, NEG_INF)
    m1 = jnp.max(le, axis=-1, keepdims=True)
    i1 = jnp.min(jnp.where(le == m1, lanef, 1e9), axis=-1, keepdims=True)
    le2 = jnp.where(lanef == i1, NEG_INF, le)
    m2 = jnp.max(le2, axis=-1, keepdims=True)
    i2 = jnp.min(jnp.where(le2 == m2, lanef, 1e9), axis=-1, keepdims=True)
    e2 = jnp.exp(m2 - m1)
    den = 1.0 + e2
    return jnp.where(lanef == i1, p_grp / den, 0.0) + jnp.where(lanef == i2, p_grp * e2 / den, 0.0)


def _moe_kernel(*refs, has_proj, tm):
    if has_proj:
        h_ref, att_ref, wo_ref = refs[:3]
        refs = refs[3:]
    else:
        h_ref = refs[0]
        refs = refs[1:]
    g_ref, wr_ref, br_ref, wg_ref, wu_ref, wd_ref, o_ref, xn_s, comb_s, acc_s = refs
    gi = pl.program_id(1)

    @pl.when(gi == 0)
    def _():
        hm = h_ref[...]
        if has_proj:
            hm = hm + _mm(att_ref[...].astype(F32), wo_ref)
        acc_s[...] = hm
        xn = _rms(hm, g_ref[...])
        xn_s[...] = xn.astype(BF16)
        logits = jnp.dot(xn, wr_ref[...], precision=lax.Precision.HIGHEST, preferred_element_type=F32)
        comb_s[...] = _route(logits + br_ref[...])

    x = xn_s[...]
    lane = lax.broadcasted_iota(jnp.int32, (tm, LANES), 1)
    comb = comb_s[...]
    for e in range(EXPERTS_PER_GROUP):
        c = jnp.sum(jnp.where(lane == gi * EXPERTS_PER_GROUP + e, comb, 0.0), axis=-1, keepdims=True)
        hg = _dot(x, wg_ref[e])
        hh = hg * _sigmoid(hg) * _dot(x, wu_ref[e]) * c
        acc_s[...] += _dot(hh.astype(BF16), wd_ref[e])

    @pl.when(gi == N_GROUPS - 1)
    def _():
        o_ref[...] = acc_s[...]


def _moe_call(h2, layer, norm_g, wr, br, wg, wu, wd, att=None, w_o=None, j=0, planes=1, *, tm):
    n = h2.shape[0]
    has_proj = att is not None
    kern = functools.partial(_moe_kernel, has_proj=has_proj, tm=tm)
    tok = pl.BlockSpec((tm, D_MODEL), lambda i, g: (i, 0))
    in_specs = [tok]
    args = [h2]
    if has_proj:
        in_specs += [tok, pl.BlockSpec((None, planes, D_MODEL, D_MODEL), lambda i, g: (j, 0, 0, 0))]
        args += [att, w_o]
    in_specs += [
        pl.BlockSpec((None, 1, D_MODEL), lambda i, g: (layer, 0, 0)),
        pl.BlockSpec((None, D_MODEL, LANES), lambda i, g: (layer, 0, 0)),
        pl.BlockSpec((None, 1, LANES), lambda i, g: (layer, 0, 0)),
        pl.BlockSpec((None, None, EXPERTS_PER_GROUP, D_MODEL, D_EXPERT), lambda i, g: (layer, g, 0, 0, 0)),
        pl.BlockSpec((None, None, EXPERTS_PER_GROUP, D_MODEL, D_EXPERT), lambda i, g: (layer, g, 0, 0, 0)),
        pl.BlockSpec((None, None, EXPERTS_PER_GROUP, D_EXPERT, D_MODEL), lambda i, g: (layer, g, 0, 0, 0)),
    ]
    args += [norm_g, wr, br, wg, wu, wd]
    return pl.pallas_call(
        kern,
        grid=(n // tm, N_GROUPS),
        in_specs=in_specs,
        out_specs=tok,
        out_shape=jax.ShapeDtypeStruct((n, D_MODEL), F32),
        scratch_shapes=[
            pltpu.VMEM((tm, D_MODEL), BF16),
            pltpu.VMEM((tm, LANES), F32),
            pltpu.VMEM((tm, D_MODEL), F32),
        ],
        compiler_params=_params("arbitrary", "arbitrary"),
        name="moe_proj" if has_proj else "moe",
    )(*args)


def _rope_tables(pos):
    half = HEAD_DIM // 2
    inv = ROPE_THETA ** (-(jnp.arange(half, dtype=F32) * 2.0 / HEAD_DIM))
    ang = pos.astype(F32)[:, None] * inv[None, :]
    cos = jnp.cos(ang)
    sin = jnp.sin(ang)
    cos_h = jnp.concatenate([cos, cos], axis=-1)
    sin_h = jnp.concatenate([-sin, sin], axis=-1)
    return jnp.concatenate([cos_h, cos_h], axis=-1), jnp.concatenate([sin_h, sin_h], axis=-1)


def kernel(x_prompt, x_sample, cache_k, cache_v, page_table, state_conv_a, state_conv_b, norm_mix_g, norm_ffn_g,
           w_in_ab, conv_a_w, conv_a_b, ln_a_g, ln_a_b, conv_b_w, w_out_ab, w_qkv, q_norm_g, k_norm_g, w_o,
           router_g_w, router_g_b, router_e_w, router_e_b, w_gate, w_up, w_down):
    bsz, s, _ = x_prompt.shape
    db, t, _ = x_sample.shape
    depth = norm_mix_g.shape[0]
    n_pages = page_table.shape[1]
    past = n_pages * PAGE_SIZE
    n_pool = cache_k.shape[1]

    w_in_b = _planes(w_in_ab, True)
    w_out_b = _planes(w_out_ab, True)
    w_qkv_b = _planes(w_qkv, True)
    w_o_b = _planes(w_o, True)
    wg_b = w_gate.astype(BF16).reshape(depth, N_GROUPS, EXPERTS_PER_GROUP, D_MODEL, D_EXPERT)
    wu_b = w_up.astype(BF16).reshape(depth, N_GROUPS, EXPERTS_PER_GROUP, D_MODEL, D_EXPERT)
    wd_b = w_down.astype(BF16).reshape(depth, N_GROUPS, EXPERTS_PER_GROUP, D_EXPERT, D_MODEL)
    pad = LANES - N_EXPERTS - N_GROUPS
    wr = jnp.concatenate([router_e_w, router_g_w, jnp.zeros((depth, D_MODEL, pad), F32)], axis=-1)
    br = jnp.concatenate([router_e_b, router_g_b, jnp.zeros((depth, pad), F32)], axis=-1)[:, None, :]
    qg = jnp.concatenate([q_norm_g, q_norm_g], axis=-1)[:, None, :]
    kg = jnp.concatenate([k_norm_g, k_norm_g], axis=-1)[:, None, :]
    norm_mix_g = norm_mix_g[:, None, :]
    norm_ffn_g = norm_ffn_g[:, None, :]
    conv_a_b = conv_a_b[:, None, :]
    ln_a_g = ln_a_g[:, None, :]
    ln_a_b = ln_a_b[:, None, :]
    cos_p, sin_p = _rope_tables(jnp.arange(s, dtype=jnp.int32))
    cos_s, sin_s = _rope_tables(past + jnp.arange(t, dtype=jnp.int32))
    cos_s = jnp.tile(cos_s, (db, 1))
    sin_s = jnp.tile(sin_s, (db, 1))
    ck = jnp.transpose(cache_k, (0, 1, 3, 4, 2)).reshape(cache_k.shape[0], n_pool, D_MODEL, PAGE_SIZE)
    cv = jnp.transpose(cache_v, (0, 1, 3, 4, 2)).reshape(cache_v.shape[0], n_pool, D_MODEL, PAGE_SIZE)

    hp, hs = x_prompt, x_sample
    n_attn = depth // 2
    kt_all = jnp.zeros((n_attn, bsz * s // PAGE_SIZE, D_MODEL, PAGE_SIZE), F32)
    vt_all = jnp.zeros((n_attn, bsz * s // PAGE_SIZE, D_MODEL, PAGE_SIZE), F32)
    ks_l, vs_l = [], []
    cap_l, cbp_l, cas_l, cbs_l = [], [], [], []
    moe_w = (norm_ffn_g, wr, br, wg_b, wu_b, wd_b)
    for layer in range(depth):
        j = layer // 2
        pp = 2 if layer < HI_PRECISION_LAYERS else 1
        if layer % 2 == 0:
            ab_w = (norm_mix_g, w_in_b, conv_a_w, conv_a_b, ln_a_g, ln_a_b, conv_b_w, w_out_b)
            zero_a = jnp.zeros((bsz, HALO_A, D_CONV), F32)
            zero_b = jnp.zeros((bsz, HALO_B, D_CONV), F32)
            hp, a_p, b_p = _ab_call(hp, zero_a, zero_b, layer, j, *ab_w, bb=1, t=256, planes=pp)
            prev_a = jnp.pad(state_conv_a[j], ((0, 0), (HALO_A - (CONV_A_WIDTH - 1), 0), (0, 0)))
            prev_b = jnp.pad(state_conv_b[j], ((0, 0), (HALO_B - (CONV_B_WIDTH - 1), 0), (0, 0)))
            hs, a_s, b_s = _ab_call(hs, prev_a, prev_b, layer, j, *ab_w, bb=db, t=t, planes=pp)
            cap_l.append(a_p[:, HALO_A - (CONV_A_WIDTH - 1):])
            cbp_l.append(b_p[:, HALO_B - (CONV_B_WIDTH - 1):])
            cas_l.append(a_s[:, HALO_A - (CONV_A_WIDTH - 1):])
            cbs_l.append(b_s[:, HALO_B - (CONV_B_WIDTH - 1):])
            hp = _moe_call(hp.reshape(bsz * s, D_MODEL), layer, *moe_w, tm=512).reshape(bsz, s, D_MODEL)
            hs = _moe_call(hs.reshape(db * t, D_MODEL), layer, *moe_w, tm=db * t).reshape(db, t, D_MODEL)
        else:
            q, k, v, kt_all, vt_all = _qkv_call(hp.reshape(bsz * s, D_MODEL), layer, j, norm_mix_g, w_qkv_b, qg, kg,
                                                cos_p, sin_p, t=256, planes=pp, paged_kv=(kt_all, vt_all))
            att = _attn_p_call(q.reshape(bsz, s, D_MODEL), k.reshape(bsz, s, D_MODEL), v.reshape(bsz, s, D_MODEL),
                               F32 if pp == 2 else BF16)
            hp = _moe_call(hp.reshape(bsz * s, D_MODEL), layer, *moe_w, att=att.reshape(bsz * s, D_MODEL),
                           w_o=w_o_b, j=j, planes=pp, tm=512).reshape(bsz, s, D_MODEL)
            q, k, v = _qkv_call(hs.reshape(db * t, D_MODEL), layer, j, norm_mix_g, w_qkv_b, qg, kg,
                                cos_s, sin_s, t=db * t, planes=pp)
            att = _attn_s_call(page_table, q.reshape(db, t, D_MODEL), k.reshape(db, t, D_MODEL),
                               v.reshape(db, t, D_MODEL), ck, cv, j)
            ks_l.append(k.reshape(db, t, N_HEADS, HEAD_DIM))
            vs_l.append(v.reshape(db, t, N_HEADS, HEAD_DIM))
            hs = _moe_call(hs.reshape(db * t, D_MODEL), layer, *moe_w, att=att.reshape(db * t, D_MODEL),
                           w_o=w_o_b, j=j, planes=pp, tm=db * t).reshape(db, t, D_MODEL)

    def unpage(a):
        return jnp.transpose(a.reshape(a.shape[0], a.shape[1], N_HEADS, HEAD_DIM, PAGE_SIZE), (0, 1, 4, 2, 3))

    return (hp, hs, unpage(kt_all), unpage(vt_all), jnp.stack(ks_l), jnp.stack(vs_l),
            jnp.stack(cap_l), jnp.stack(cbp_l), jnp.stack(cas_l), jnp.stack(cbs_l))
```
